```python
import jax, jax.numpy as jnp
from jax import lax
import numpy as np

D_MODEL = 4096
BATCH = 4
SEQ = 2048
DEPTH = 2
DEC_BATCH = 32
DEC_SEQ = 4
PAST_LEN = 16384
PAGE_SIZE = 128

N_HEADS = 32
N_KV_HEADS = 8
HEAD_DIM = D_MODEL // N_HEADS
GROUP = N_HEADS // N_KV_HEADS
QKV_DIM = (N_HEADS + 2 * N_KV_HEADS) * HEAD_DIM
D_FF = 11008
WINDOW = 128
MOBA_BLOCK = 256
MOBA_TOPK = 3
MOBA_Q_CHUNK = 16
ROPE_THETA = 10000.0
RMS_EPS = 1e-6
SCALE = HEAD_DIM ** -0.5
N_MIXERS = 2
N_SWA_LAYERS = (DEPTH + 1) // N_MIXERS
N_MOBA_LAYERS = DEPTH // N_MIXERS

kernel_name = "hybrid_swa_sink_moba_macaron_step"


def rms_norm(x, g):
    xf = x.astype(jnp.float32)
    y = xf * lax.rsqrt(jnp.mean(xf * xf, axis=-1, keepdims=True) + RMS_EPS)
    return y.astype(x.dtype) * g


def swiglu(h, w_gate, w_up, w_down):
    return (jax.nn.silu(h @ w_gate) * (h @ w_up)) @ w_down


def rope(x, pos):
    inv = ROPE_THETA ** (-jnp.arange(0, HEAD_DIM, 2, dtype=jnp.float32) / HEAD_DIM)
    ang = pos.astype(jnp.float32)[:, None] * inv[None, :]
    cos = jnp.cos(ang)[:, None, :]
    sin = jnp.sin(ang)[:, None, :]
    xf = x.astype(jnp.float32)
    x1, x2 = xf[..., : HEAD_DIM // 2], xf[..., HEAD_DIM // 2:]
    return jnp.concatenate([x1 * cos - x2 * sin, x2 * cos + x1 * sin], axis=-1).astype(x.dtype)


def project_qkv(h, w_qkv, pos):
    b, t, _ = h.shape
    qkv = h @ w_qkv
    q = qkv[..., : N_HEADS * HEAD_DIM].reshape(b, t, N_HEADS, HEAD_DIM)
    k = qkv[..., N_HEADS * HEAD_DIM:(N_HEADS + N_KV_HEADS) * HEAD_DIM].reshape(b, t, N_KV_HEADS, HEAD_DIM)
    v = qkv[..., (N_HEADS + N_KV_HEADS) * HEAD_DIM:].reshape(b, t, N_KV_HEADS, HEAD_DIM)
    q = rope(q, pos).reshape(b, t, N_KV_HEADS, GROUP, HEAD_DIM)
    k = rope(k, pos)
    return q, k, v


def out_proj(o, w_o):
    b, t = o.shape[:2]
    return o.reshape(b, t, N_HEADS * HEAD_DIM) @ w_o


def masked_softmax(s, mask, sink=None):
    s = jnp.where(mask, s.astype(jnp.float32), -jnp.inf)
    m = jnp.max(s, axis=-1, keepdims=True)
    if sink is not None:
        sk = sink.astype(jnp.float32)
        m = jnp.maximum(m, sk)
    p = jnp.exp(s - m)
    denom = jnp.sum(p, axis=-1, keepdims=True)
    if sink is not None:
        denom = denom + jnp.exp(sk - m)
    return p / denom


def window_attend(q, k, v, q_pos, k_pos, sink):
    s = jnp.einsum('...igjd,...kgd->...gjik', q, k) * SCALE
    qp = q_pos[..., :, None]
    kp = k_pos[..., None, :]
    mask = (kp <= qp) & (kp > qp - WINDOW) & (kp >= 0)
    p = masked_softmax(s, mask[..., None, None, :, :], sink.reshape(N_KV_HEADS, GROUP, 1, 1))
    return jnp.einsum('...gjik,...kgd->...igjd', p.astype(v.dtype), v)


def swa_prompt(q, k, v, sink):
    b, t = q.shape[:2]
    nb = t // WINDOW
    qb = q.reshape(b, nb, WINDOW, N_KV_HEADS, GROUP, HEAD_DIM)

    def band(x):
        xp = jnp.pad(x, ((0, 0), (WINDOW, 0), (0, 0), (0, 0))).reshape(b, nb + 1, WINDOW, N_KV_HEADS, HEAD_DIM)
        return jnp.concatenate([xp[:, :-1], xp[:, 1:]], axis=2)

    blk = jnp.arange(nb)[:, None]
    q_pos = blk * WINDOW + jnp.arange(WINDOW)[None, :]
    k_pos = (blk - 1) * WINDOW + jnp.arange(2 * WINDOW)[None, :]
    o = window_attend(qb, band(k), band(v), q_pos, k_pos, sink)
    return o.reshape(b, t, N_KV_HEADS, GROUP, HEAD_DIM)


def swa_sample(q, k_new, v_new, buf_k, buf_v, sink):
    t = q.shape[1]
    keys = jnp.concatenate([buf_k, k_new], axis=1)
    vals = jnp.concatenate([buf_v, v_new], axis=1)
    q_pos = PAST_LEN + jnp.arange(t)
    k_pos = PAST_LEN - WINDOW + jnp.arange(WINDOW + t)
    o = window_attend(q, keys, vals, q_pos, k_pos, sink)
    return o, keys[:, t:], vals[:, t:]


def moba_seq(q, k, v, q_pos, chunk):
    length = k.shape[0]
    nb = -(-length // MOBA_BLOCK)
    pad = nb * MOBA_BLOCK - length
    kb = jnp.pad(k, ((0, pad), (0, 0), (0, 0))).reshape(nb, MOBA_BLOCK, N_KV_HEADS, HEAD_DIM).transpose(2, 0, 1, 3)
    vb = jnp.pad(v, ((0, pad), (0, 0), (0, 0))).reshape(nb, MOBA_BLOCK, N_KV_HEADS, HEAD_DIM).transpose(2, 0, 1, 3)
    means = jnp.mean(kb.astype(jnp.float32), axis=2)
    k_top = min(MOBA_TOPK, nb)
    g_idx = jnp.arange(N_KV_HEADS)[None, :, None, None]
    blk = jnp.arange(nb)

    def attend_chunk(args):
        qc, pc = args
        c = qc.shape[0]
        cur = (pc // MOBA_BLOCK)[:, None, None, None]
        gate = jnp.einsum('cgjd,gnd->cgjn', qc.astype(jnp.float32), means)
        gate = jnp.where(blk[None, None, None, :] < cur, gate, -jnp.inf)
        _, top = lax.top_k(gate, k_top)
        own = jnp.broadcast_to(cur, top.shape[:-1] + (1,))
        sel = jnp.concatenate([top, own], axis=-1)
        sel_ok = jnp.concatenate([top < cur, jnp.ones(own.shape, dtype=bool)], axis=-1)
        kg = kb[g_idx, sel]
        vg = vb[g_idx, sel]
        s = jnp.einsum('cgjd,cgjsnd->cgjsn', qc, kg) * SCALE
        k_pos = sel[..., None] * MOBA_BLOCK + jnp.arange(MOBA_BLOCK)
        mask = sel_ok[..., None] & (k_pos <= pc[:, None, None, None, None])
        p = masked_softmax(s.reshape(c, N_KV_HEADS, GROUP, -1), mask.reshape(c, N_KV_HEADS, GROUP, -1)).reshape(s.shape)
        return jnp.einsum('cgjsn,cgjsnd->cgjd', p.astype(vg.dtype), vg)

    tq = q.shape[0]
    out = lax.map(attend_chunk, (q.reshape(tq // chunk, chunk, N_KV_HEADS, GROUP, HEAD_DIM), q_pos.reshape(tq // chunk, chunk)))
    return out.reshape(tq, N_KV_HEADS, GROUP, HEAD_DIM)


def moba_prompt(q, k, v):
    t = q.shape[1]
    pos = jnp.arange(t)
    return lax.map(lambda a: moba_seq(a[0], a[1], a[2], pos, MOBA_Q_CHUNK), (q, k, v))


def moba_sample(q, k_new, v_new, pool_k, pool_v, page_table):
    t = q.shape[1]
    pos = PAST_LEN + jnp.arange(t)

    def one(args):
        pt, qb, kn, vn = args
        k_all = jnp.concatenate([pool_k[pt].reshape(-1, N_KV_HEADS, HEAD_DIM), kn], axis=0)
        v_all = jnp.concatenate([pool_v[pt].reshape(-1, N_KV_HEADS, HEAD_DIM), vn], axis=0)
        return moba_seq(qb, k_all, v_all, pos, t)

    return lax.map(one, (page_table, q, k_new, v_new))


def setup_inputs(seed: int = 0) -> dict:
    key = jax.random.key(seed)
    ks = jax.random.split(key, 20)
    n_pages = PAST_LEN // PAGE_SIZE
    n_used = DEC_BATCH * n_pages
    n_pool = n_used + max(1, n_used // 4)

    def w(k, shape, fan_in):
        return jax.random.normal(k, shape, jnp.float32) * fan_in ** -0.5

    def gain(k, shape):
        return 1.0 + 0.01 * jax.random.normal(k, shape, jnp.float32)

    page_table = jax.random.permutation(ks[6], n_pool)[:n_used].reshape(DEC_BATCH, n_pages).astype(jnp.int32)
    return {
        "x_prompt": jax.random.normal(ks[0], (BATCH, SEQ, D_MODEL), jnp.float32),
        "x_sample": jax.random.normal(ks[1], (DEC_BATCH, DEC_SEQ, D_MODEL), jnp.float32),
        "cache_swa_k": jax.random.normal(ks[2], (N_SWA_LAYERS, DEC_BATCH, WINDOW, N_KV_HEADS, HEAD_DIM), jnp.float32),
        "cache_swa_v": jax.random.normal(ks[3], (N_SWA_LAYERS, DEC_BATCH, WINDOW, N_KV_HEADS, HEAD_DIM), jnp.float32),
        "cache_moba_k": jax.random.normal(ks[4], (N_MOBA_LAYERS, n_pool, PAGE_SIZE, N_KV_HEADS, HEAD_DIM), jnp.float32),
        "cache_moba_v": jax.random.normal(ks[5], (N_MOBA_LAYERS, n_pool, PAGE_SIZE, N_KV_HEADS, HEAD_DIM), jnp.float32),
        "page_table": page_table,
        "attn_norm": gain(ks[7], (DEPTH, D_MODEL)),
        "w_qkv": w(ks[8], (DEPTH, D_MODEL, QKV_DIM), D_MODEL),
        "w_o": w(ks[9], (DEPTH, N_HEADS * HEAD_DIM, D_MODEL), N_HEADS * HEAD_DIM),
        "swa_sinks": 0.5 * jax.random.normal(ks[10], (N_SWA_LAYERS, N_HEADS), jnp.float32),
        "ffn1_norm": gain(ks[11], (DEPTH, D_MODEL)),
        "ffn1_w_gate": w(ks[12], (DEPTH, D_MODEL, D_FF), D_MODEL),
        "ffn1_w_up": w(ks[13], (DEPTH, D_MODEL, D_FF), D_MODEL),
        "ffn1_w_down": w(ks[14], (DEPTH, D_FF, D_MODEL), D_FF),
        "ffn2_norm": gain(ks[15], (DEPTH, D_MODEL)),
        "ffn2_w_gate": w(ks[16], (DEPTH, D_MODEL, D_FF), D_MODEL),
        "ffn2_w_up": w(ks[17], (DEPTH, D_MODEL, D_FF), D_MODEL),
        "ffn2_w_down": w(ks[18], (DEPTH, D_FF, D_MODEL), D_FF),
        "final_norm": gain(ks[19], (D_MODEL,)),
    }


def reference(x_prompt, x_sample, cache_swa_k, cache_swa_v, cache_moba_k, cache_moba_v, page_table,
              attn_norm, w_qkv, w_o, swa_sinks,
              ffn1_norm, ffn1_w_gate, ffn1_w_up, ffn1_w_down,
              ffn2_norm, ffn2_w_gate, ffn2_w_up, ffn2_w_down, final_norm):
    pos_p = jnp.arange(x_prompt.shape[1])
    pos_s = PAST_LEN + jnp.arange(x_sample.shape[1])
    xp, xs = x_prompt, x_sample
    swa_kp, swa_vp, swa_ks, swa_vs = [], [], [], []
    moba_kp, moba_vp, moba_ks, moba_vs = [], [], [], []
    for i in range(DEPTH):
        a = i // N_MIXERS
        xp = xp + 0.5 * swiglu(rms_norm(xp, ffn1_norm[i]), ffn1_w_gate[i], ffn1_w_up[i], ffn1_w_down[i])
        xs = xs + 0.5 * swiglu(rms_norm(xs, ffn1_norm[i]), ffn1_w_gate[i], ffn1_w_up[i], ffn1_w_down[i])
        qp, kp, vp = project_qkv(rms_norm(xp, attn_norm[i]), w_qkv[i], pos_p)
        qs, ks_, vs_ = project_qkv(rms_norm(xs, attn_norm[i]), w_qkv[i], pos_s)
        if i % N_MIXERS == 0:
            op = swa_prompt(qp, kp, vp, swa_sinks[a])
            os_, nk, nv = swa_sample(qs, ks_, vs_, cache_swa_k[a], cache_swa_v[a], swa_sinks[a])
            swa_kp.append(kp[:, -WINDOW:])
            swa_vp.append(vp[:, -WINDOW:])
            swa_ks.append(nk)
            swa_vs.append(nv)
        else:
            op = moba_prompt(qp, kp, vp)
            os_ = moba_sample(qs, ks_, vs_, cache_moba_k[a], cache_moba_v[a], page_table)
            moba_kp.append(kp)
            moba_vp.append(vp)
            moba_ks.append(ks_)
            moba_vs.append(vs_)
        xp = xp + out_proj(op, w_o[i])
        xs = xs + out_proj(os_, w_o[i])
        xp = xp + 0.5 * swiglu(rms_norm(xp, ffn2_norm[i]), ffn2_w_gate[i], ffn2_w_up[i], ffn2_w_down[i])
        xs = xs + 0.5 * swiglu(rms_norm(xs, ffn2_norm[i]), ffn2_w_gate[i], ffn2_w_up[i], ffn2_w_down[i])
    y_prompt = rms_norm(xp, final_norm)
    y_sample = rms_norm(xs, final_norm)
    new_swa_k_prompt = jnp.stack(swa_kp)
    new_swa_v_prompt = jnp.stack(swa_vp)
    new_swa_k_sample = jnp.stack(swa_ks)
    new_swa_v_sample = jnp.stack(swa_vs)
    new_moba_k_prompt = jnp.stack(moba_kp)
    new_moba_v_prompt = jnp.stack(moba_vp)
    new_moba_k_sample = jnp.stack(moba_ks)
    new_moba_v_sample = jnp.stack(moba_vs)
    return (y_prompt, y_sample, new_swa_k_prompt, new_swa_v_prompt, new_swa_k_sample, new_swa_v_sample,
            new_moba_k_prompt, new_moba_v_prompt, new_moba_k_sample, new_moba_v_sample)
```

```python
import functools

import jax
import jax.numpy as jnp
from jax import lax
from jax.experimental import pallas as pl
from jax.experimental.pallas import tpu as pltpu

F32 = jnp.float32
BF16 = jnp.bfloat16

D_MODEL = 4096
N_HEADS = 32
N_KV_HEADS = 8
HEAD_DIM = 128
GROUP = N_HEADS // N_KV_HEADS
D_FF = 11008
WINDOW = 128
MOBA_BLOCK = 256
MOBA_TOPK = 3
PAGE_SIZE = 128
PAST_LEN = 16384
DEC_SEQ = 4
ROPE_THETA = 10000.0
RMS_EPS = 1e-6
SCALE = HEAD_DIM ** -0.5

V7X_VMEM_BYTES = 64 * 1024 * 1024
VMEM_LIMIT = V7X_VMEM_BYTES - 8 * 1024 * 1024
LANES = 128

ROW_TILE = 1040
NORM_TILE = 416
FF_TILE = 512
D_FF_PAD = 11264
DOWN_TN = 2048
DOWN_TK = 1024
QKV_TN = 1024
OUT_TN = 1024
OUT_TK = 2048
PAGES_PER_CHUNK = 4
CHUNK_SLOTS = 3
ROWS_PER_KV = GROUP * DEC_SEQ
NEG_INF = float("-inf")
NT_DIMS = (((1,), (1,)), ((), ()))


def _params(*sem):
    return pltpu.CompilerParams(dimension_semantics=sem, vmem_limit_bytes=VMEM_LIMIT)


def _rms_kernel(x_ref, g_ref, o_ref):
    x = x_ref[...]
    ms = jnp.mean(x * x, axis=-1, keepdims=True)
    o_ref[...] = ((x * lax.rsqrt(ms + RMS_EPS)) * g_ref[...]).astype(o_ref.dtype)


def rmsnorm(x, gain, out_dtype):
    m, d = x.shape
    return pl.pallas_call(
        _rms_kernel,
        grid=(m // NORM_TILE,),
        in_specs=[pl.BlockSpec((NORM_TILE, d), lambda i: (i, 0)),
                  pl.BlockSpec((1, d), lambda i: (0, 0))],
        out_specs=pl.BlockSpec((NORM_TILE, d), lambda i: (i, 0)),
        out_shape=jax.ShapeDtypeStruct((m, d), out_dtype),
        compiler_params=_params("parallel"),
        name="rmsnorm",
    )(x, gain.reshape(1, d))


def _gateup_kernel(h_ref, wg_ref, wu_ref, o_ref):
    h = h_ref[...]
    g = jnp.dot(h, wg_ref[...], preferred_element_type=F32)
    u = jnp.dot(h, wu_ref[...], preferred_element_type=F32)
    o_ref[...] = (g * jax.nn.sigmoid(g) * u).astype(o_ref.dtype)


def gate_up(h, wg, wu):
    m, d = h.shape
    f = wg.shape[1]
    return pl.pallas_call(
        _gateup_kernel,
        grid=(m // ROW_TILE, f // FF_TILE),
        in_specs=[pl.BlockSpec((ROW_TILE, d), lambda i, j: (i, 0)),
                  pl.BlockSpec((d, FF_TILE), lambda i, j: (0, j)),
                  pl.BlockSpec((d, FF_TILE), lambda i, j: (0, j))],
        out_specs=pl.BlockSpec((ROW_TILE, FF_TILE), lambda i, j: (i, j)),
        out_shape=jax.ShapeDtypeStruct((m, f), BF16),
        compiler_params=_params("parallel", "arbitrary"),
        name="gate_up",
    )(h, wg, wu)


def _matmul_residual_kernel(a_ref, w_ref, r_ref, o_ref, *, scale):
    k = pl.program_id(2)
    p = scale * jnp.dot(a_ref[...], w_ref[...], preferred_element_type=F32)

    @pl.when(k == 0)
    def _():
        o_ref[...] = r_ref[...] + p

    @pl.when(k > 0)
    def _():
        o_ref[...] += p


def matmul_residual(a, w, res, scale, tn, tk):
    m, kdim = a.shape
    n = w.shape[1]
    return pl.pallas_call(
        functools.partial(_matmul_residual_kernel, scale=scale),
        grid=(m // ROW_TILE, n // tn, kdim // tk),
        in_specs=[pl.BlockSpec((ROW_TILE, tk), lambda i, j, k: (i, k)),
                  pl.BlockSpec((tk, tn), lambda i, j, k: (k, j)),
                  pl.BlockSpec((ROW_TILE, tn), lambda i, j, k: (i, j))],
        out_specs=pl.BlockSpec((ROW_TILE, tn), lambda i, j, k: (i, j)),
        out_shape=jax.ShapeDtypeStruct((m, n), F32),
        compiler_params=_params("parallel", "parallel", "arbitrary"),
        name="matmul_residual",
    )(a, w, res)


def _qkv_kernel(h_ref, w_ref, cos_ref, sin_ref, o_ref, *, rope_tiles):
    acc = jnp.dot(h_ref[...], w_ref[...], preferred_element_type=F32)
    n = pl.program_id(1)

    @pl.when(n < rope_tiles)
    def _():
        cosf = cos_ref[...]
        sinf = sin_ref[...]
        for c in range(acc.shape[1] // HEAD_DIM):
            x = acc[:, c * HEAD_DIM:(c + 1) * HEAD_DIM]
            o_ref[:, c * HEAD_DIM:(c + 1) * HEAD_DIM] = x * cosf + pltpu.roll(x, HEAD_DIM // 2, 1) * sinf

    @pl.when(n >= rope_tiles)
    def _():
        o_ref[...] = acc


def qkv_rope(h, w, cosf, sinf):
    m, d = h.shape
    n = w.shape[1]
    rope_tiles = (N_HEADS + N_KV_HEADS) * HEAD_DIM // QKV_TN
    return pl.pallas_call(
        functools.partial(_qkv_kernel, rope_tiles=rope_tiles),
        grid=(m // ROW_TILE, n // QKV_TN),
        in_specs=[pl.BlockSpec((ROW_TILE, d), lambda i, j: (i, 0)),
                  pl.BlockSpec((d, QKV_TN), lambda i, j: (0, j)),
                  pl.BlockSpec((ROW_TILE, HEAD_DIM), lambda i, j: (i, 0)),
                  pl.BlockSpec((ROW_TILE, HEAD_DIM), lambda i, j: (i, 0))],
        out_specs=pl.BlockSpec((ROW_TILE, QKV_TN), lambda i, j: (i, j)),
        out_shape=jax.ShapeDtypeStruct((m, n), F32),
        compiler_params=_params("parallel", "arbitrary"),
        name="qkv_rope",
    )(h, w, cosf, sinf)


def rope_tables(pos):
    inv = ROPE_THETA ** (-jnp.arange(0, HEAD_DIM, 2, dtype=F32) / HEAD_DIM)
    ang = pos.astype(F32)[:, None] * inv[None, :]
    cos = jnp.cos(ang)
    sin = jnp.sin(ang)
    return jnp.concatenate([cos, cos], axis=-1), jnp.concatenate([-sin, sin], axis=-1)


def _swa_prompt_kernel(sink_ref, q_ref, kp_ref, kc_ref, vp_ref, vc_ref, o_ref):
    g = pl.program_id(1)
    n = pl.program_id(2)
    k = jnp.concatenate([kp_ref[...], kc_ref[...]], axis=0).astype(BF16)
    v = jnp.concatenate([vp_ref[...], vc_ref[...]], axis=0).astype(BF16)
    r = lax.broadcasted_iota(jnp.int32, (WINDOW, 2 * WINDOW), 0)
    c = lax.broadcasted_iota(jnp.int32, (WINDOW, 2 * WINDOW), 1)
    mask = (c > r) & (c <= r + WINDOW) & ((c >= WINDOW) | (n > 0))
    for j in range(GROUP):
        q = q_ref[:, j * HEAD_DIM:(j + 1) * HEAD_DIM].astype(BF16)
        s = lax.dot_general(q, k, NT_DIMS, preferred_element_type=F32) * SCALE
        s = jnp.where(mask, s, NEG_INF)
        sink = sink_ref[g * GROUP + j]
        m = jnp.maximum(jnp.max(s, axis=-1, keepdims=True), sink)
        p = jnp.exp(s - m)
        denom = jnp.sum(p, axis=-1, keepdims=True) + jnp.exp(sink - m)
        o = jnp.dot(p.astype(BF16), v, preferred_element_type=F32) / denom
        o_ref[:, j * HEAD_DIM:(j + 1) * HEAD_DIM] = o.astype(o_ref.dtype)


def swa_prompt(qkv, sinks, batch, seq):
    nb = seq // WINDOW
    kcol = N_HEADS
    vcol = N_HEADS + N_KV_HEADS
    qw = GROUP * HEAD_DIM
    return pl.pallas_call(
        _swa_prompt_kernel,
        grid=(batch, N_KV_HEADS, nb),
        in_specs=[
            pl.BlockSpec(memory_space=pltpu.SMEM),
            pl.BlockSpec((WINDOW, qw), lambda b, g, n: (b * nb + n, g)),
            pl.BlockSpec((WINDOW, HEAD_DIM), lambda b, g, n: (b * nb + jnp.maximum(n - 1, 0), kcol + g)),
            pl.BlockSpec((WINDOW, HEAD_DIM), lambda b, g, n: (b * nb + n, kcol + g)),
            pl.BlockSpec((WINDOW, HEAD_DIM), lambda b, g, n: (b * nb + jnp.maximum(n - 1, 0), vcol + g)),
            pl.BlockSpec((WINDOW, HEAD_DIM), lambda b, g, n: (b * nb + n, vcol + g)),
        ],
        out_specs=pl.BlockSpec((WINDOW, qw), lambda b, g, n: (b * nb + n, g)),
        out_shape=jax.ShapeDtypeStruct((batch * seq, N_HEADS * HEAD_DIM), BF16),
        compiler_params=_params("parallel", "parallel", "arbitrary"),
        name="swa_prompt",
    )(sinks, qkv, qkv, qkv, qkv, qkv)


def _moba_prompt_kernel(q_ref, k_ref, v_ref, o_ref, means_ref, *, nblk):
    c = pl.program_id(2)

    @pl.when(c == 0)
    def _():
        means_ref[...] = jnp.zeros_like(means_ref)
        for n in range(nblk):
            kb = k_ref[n * MOBA_BLOCK:(n + 1) * MOBA_BLOCK, :]
            means_ref[n:n + 1, :] = jnp.sum(kb, axis=0, keepdims=True) / MOBA_BLOCK

    lane = lax.broadcasted_iota(jnp.int32, (MOBA_BLOCK, LANES), 1)
    r = lax.broadcasted_iota(jnp.int32, (MOBA_BLOCK, MOBA_BLOCK), 0)
    cc = lax.broadcasted_iota(jnp.int32, (MOBA_BLOCK, MOBA_BLOCK), 1)
    off = pl.multiple_of(c * MOBA_BLOCK, MOBA_BLOCK)
    k_own = k_ref[pl.ds(off, MOBA_BLOCK), :].astype(BF16)
    v_own = v_ref[pl.ds(off, MOBA_BLOCK), :].astype(BF16)
    for j in range(GROUP):
        qf = q_ref[:, j * HEAD_DIM:(j + 1) * HEAD_DIM]
        q = qf.astype(BF16)
        gate = lax.dot_general(qf, means_ref[...], NT_DIMS,
                               precision=lax.Precision.HIGHEST, preferred_element_type=F32)
        gate = jnp.where(lane < c, gate, NEG_INF)

        s = lax.dot_general(q, k_own, NT_DIMS, preferred_element_type=F32) * SCALE
        s = jnp.where(cc <= r, s, NEG_INF)
        m0 = jnp.max(s, axis=-1, keepdims=True)
        p = jnp.exp(s - m0)
        l0 = jnp.sum(p, axis=-1, keepdims=True)
        acc0 = jnp.dot(p.astype(BF16), v_own, preferred_element_type=F32)

        def past_block(n, carry):
            m, l, acc = carry
            gate_n = jnp.sum(jnp.where(lane == n, gate, 0.0), axis=-1, keepdims=True)
            ahead = (gate > gate_n) | ((gate == gate_n) & (lane < n))
            rank = jnp.sum(jnp.where(ahead, 1.0, 0.0), axis=-1, keepdims=True)
            selected = rank < MOBA_TOPK
            noff = pl.multiple_of(n * MOBA_BLOCK, MOBA_BLOCK)
            kb = k_ref[pl.ds(noff, MOBA_BLOCK), :].astype(BF16)
            vb = v_ref[pl.ds(noff, MOBA_BLOCK), :].astype(BF16)
            sb = lax.dot_general(q, kb, NT_DIMS, preferred_element_type=F32) * SCALE
            sb = jnp.where(selected, sb, NEG_INF)
            m_new = jnp.maximum(m, jnp.max(sb, axis=-1, keepdims=True))
            alpha = jnp.exp(m - m_new)
            pb = jnp.exp(sb - m_new)
            l_new = alpha * l + jnp.sum(pb, axis=-1, keepdims=True)
            acc_new = alpha * acc + jnp.dot(pb.astype(BF16), vb, preferred_element_type=F32)
            return m_new, l_new, acc_new

        _, l, acc = lax.fori_loop(0, c, past_block, (m0, l0, acc0))
        o_ref[:, j * HEAD_DIM:(j + 1) * HEAD_DIM] = (acc / l).astype(o_ref.dtype)


def moba_prompt(qkv, batch, seq):
    nblk = seq // MOBA_BLOCK
    kcol = N_HEADS
    vcol = N_HEADS + N_KV_HEADS
    qw = GROUP * HEAD_DIM
    return pl.pallas_call(
        functools.partial(_moba_prompt_kernel, nblk=nblk),
        grid=(batch, N_KV_HEADS, nblk),
        in_specs=[
            pl.BlockSpec((MOBA_BLOCK, qw), lambda b, g, c: (b * nblk + c, g)),
            pl.BlockSpec((seq, HEAD_DIM), lambda b, g, c: (b, kcol + g)),
            pl.BlockSpec((seq, HEAD_DIM), lambda b, g, c: (b, vcol + g)),
        ],
        out_specs=pl.BlockSpec((MOBA_BLOCK, qw), lambda b, g, c: (b * nblk + c, g)),
        out_shape=jax.ShapeDtypeStruct((batch * seq, N_HEADS * HEAD_DIM), BF16),
        scratch_shapes=[pltpu.VMEM((LANES, HEAD_DIM), F32)],
        compiler_params=_params("parallel", "parallel", "arbitrary"),
        name="moba_prompt",
    )(qkv, qkv, qkv)


def _row_token():
    return lax.broadcasted_iota(jnp.int32, (ROWS_PER_KV, LANES), 0) % DEC_SEQ


def _swa_sample_kernel(sink_ref, q_ref, kn_ref, vn_ref, ck_ref, cv_ref, o_ref, nk_ref, nv_ref):
    keep = WINDOW - DEC_SEQ
    nk_ref[0, 0:keep] = ck_ref[0, DEC_SEQ:WINDOW]
    nk_ref[0, keep:WINDOW] = kn_ref[0, 0:DEC_SEQ]
    nv_ref[0, 0:keep] = cv_ref[0, DEC_SEQ:WINDOW]
    nv_ref[0, keep:WINDOW] = vn_ref[0, 0:DEC_SEQ]

    t = lax.broadcasted_iota(jnp.int32, (ROWS_PER_KV, 2 * WINDOW), 0) % DEC_SEQ
    col = lax.broadcasted_iota(jnp.int32, (ROWS_PER_KV, 2 * WINDOW), 1)
    mask = ((col < WINDOW) & (col > t)) | ((col >= WINDOW) & (col - WINDOW <= t))
    row = lax.broadcasted_iota(jnp.int32, (ROWS_PER_KV, 1), 0)
    for g in range(N_KV_HEADS):
        q = q_ref[0, g].astype(BF16)
        k = jnp.concatenate([ck_ref[0, :, g, :], kn_ref[0, :, g, :]], axis=0).astype(BF16)
        v = jnp.concatenate([cv_ref[0, :, g, :], vn_ref[0, :, g, :]], axis=0).astype(BF16)
        s = lax.dot_general(q, k, NT_DIMS, preferred_element_type=F32) * SCALE
        s = jnp.where(mask, s, NEG_INF)
        sink = jnp.zeros((ROWS_PER_KV, 1), F32)
        for j in range(GROUP):
            sink = jnp.where(row // DEC_SEQ == j, sink_ref[g * GROUP + j], sink)
        m = jnp.maximum(jnp.max(s, axis=-1, keepdims=True), sink)
        p = jnp.exp(s - m)
        denom = jnp.sum(p, axis=-1, keepdims=True) + jnp.exp(sink - m)
        o_ref[0, g] = jnp.dot(p.astype(BF16), v, preferred_element_type=F32) / denom


def swa_sample(sinks, q_rows, k_own, v_own, cache_k, cache_v):
    nb = q_rows.shape[0]
    page_spec = pl.BlockSpec((1, PAGE_SIZE, N_KV_HEADS, HEAD_DIM), lambda b: (b, 0, 0, 0))
    q_spec = pl.BlockSpec((1, N_KV_HEADS, ROWS_PER_KV, HEAD_DIM), lambda b: (b, 0, 0, 0))
    cache_shape = jax.ShapeDtypeStruct((nb, WINDOW, N_KV_HEADS, HEAD_DIM), F32)
    return pl.pallas_call(
        _swa_sample_kernel,
        grid=(nb,),
        in_specs=[pl.BlockSpec(memory_space=pltpu.SMEM), q_spec, page_spec, page_spec, page_spec, page_spec],
        out_specs=[q_spec, page_spec, page_spec],
        out_shape=[jax.ShapeDtypeStruct(q_rows.shape, F32), cache_shape, cache_shape],
        compiler_params=_params("parallel"),
        name="swa_sample",
    )(sinks, q_rows, k_own, v_own, cache_k, cache_v)


def _moba_sample_kernel(pt_ref, q_ref, kn_ref, vn_ref, kpool, vpool, o_ref,
                        buf, sem, scores, means, sel, acc, *, n_pages):
    b = pl.program_id(0)
    n_chunks = n_pages // PAGES_PER_CHUNK
    blocks_per_chunk = PAGES_PER_CHUNK * PAGE_SIZE // MOBA_BLOCK
    n_past = n_pages * PAGE_SIZE // MOBA_BLOCK

    def chunk_copy(pool, ci, p):
        page = pt_ref[b * n_pages + ci * PAGES_PER_CHUNK + p]
        slot = ci % CHUNK_SLOTS
        return pltpu.make_async_copy(pool.at[page], buf.at[slot, pl.ds(p * PAGE_SIZE, PAGE_SIZE)],
                                     sem.at[slot, p])

    def start_chunk(pool, ci):
        for p in range(PAGES_PER_CHUNK):
            chunk_copy(pool, ci, p).start()

    def wait_chunk(pool, ci):
        for p in range(PAGES_PER_CHUNK):
            chunk_copy(pool, ci, p).wait()

    def stream(pool, per_block):
        for ci in range(CHUNK_SLOTS - 1):
            start_chunk(pool, ci)

        def body(ci, carry):
            @pl.when(ci + CHUNK_SLOTS - 1 < n_chunks)
            def _():
                start_chunk(pool, ci + CHUNK_SLOTS - 1)

            wait_chunk(pool, ci)
            slot = ci % CHUNK_SLOTS
            for h in range(blocks_per_chunk):
                per_block(ci * blocks_per_chunk + h, buf.at[slot, pl.ds(h * MOBA_BLOCK, MOBA_BLOCK)])
            return carry

        lax.fori_loop(0, n_chunks, body, 0)

    def k_block(n, rows):
        means[n] = jnp.sum(rows[...], axis=0) / MOBA_BLOCK
        for g in range(N_KV_HEADS):
            kg = rows[:, g, :].astype(BF16)
            scores[g, n] = lax.dot_general(q_ref[0, g].astype(BF16), kg, NT_DIMS,
                                           preferred_element_type=F32) * SCALE

    stream(kpool, k_block)

    t = _row_token()
    lane = lax.broadcasted_iota(jnp.int32, (ROWS_PER_KV, LANES), 1)
    stats = []
    for g in range(N_KV_HEADS):
        qf = q_ref[0, g]
        gate = lax.dot_general(qf, means[:, g, :], NT_DIMS,
                               precision=lax.Precision.HIGHEST, preferred_element_type=F32)
        blk = lax.broadcasted_iota(jnp.int32, gate.shape, 1).astype(F32)
        chosen = jnp.zeros(gate.shape, F32)
        for _ in range(MOBA_TOPK):
            best = jnp.max(gate, axis=1, keepdims=True)
            first = jnp.min(jnp.where(gate == best, blk, float(n_past)), axis=1, keepdims=True)
            hit = blk == first
            chosen = jnp.where(hit, 1.0, chosen)
            gate = jnp.where(hit, NEG_INF, gate)
        sel[g] = chosen

        s_own = lax.dot_general(qf.astype(BF16), kn_ref[0, :, g, :].astype(BF16), NT_DIMS,
                                preferred_element_type=F32) * SCALE
        s_own = jnp.where(lane <= t, s_own, NEG_INF)

        def masked_scores(n, g=g):
            blk_lane = lax.broadcasted_iota(jnp.int32, (ROWS_PER_KV, n_past), 1)
            picked = jnp.sum(jnp.where(blk_lane == n, sel[g], 0.0), axis=1, keepdims=True) > 0.0
            return jnp.where(picked, scores[g, n], NEG_INF)

        def max_body(n, mx):
            return jnp.maximum(mx, masked_scores(n))

        mx = lax.fori_loop(0, n_past, max_body, jnp.full((ROWS_PER_KV, MOBA_BLOCK), NEG_INF, F32))
        m = jnp.maximum(jnp.max(mx, axis=1, keepdims=True), jnp.max(s_own, axis=1, keepdims=True))
        p_own = jnp.exp(s_own - m)

        def exp_body(n, tot, g=g, m=m):
            p = jnp.exp(masked_scores(n) - m)
            scores[g, n] = p
            return tot + p

        tot = lax.fori_loop(0, n_past, exp_body, jnp.zeros((ROWS_PER_KV, MOBA_BLOCK), F32))
        l = jnp.sum(tot, axis=1, keepdims=True) + jnp.sum(p_own, axis=1, keepdims=True)
        stats.append(l)
        acc[g] = jnp.dot(p_own.astype(BF16), vn_ref[0, :, g, :].astype(BF16), preferred_element_type=F32)

    def v_block(n, rows):
        for g in range(N_KV_HEADS):
            acc[g] += jnp.dot(scores[g, n].astype(BF16), rows[:, g, :].astype(BF16),
                              preferred_element_type=F32)

    stream(vpool, v_block)
    for g in range(N_KV_HEADS):
        o_ref[0, g] = acc[g] / stats[g]


def moba_sample(page_table, q_rows, k_own, v_own, pool_k, pool_v):
    nb, n_pages = page_table.shape
    n_past = n_pages * PAGE_SIZE // MOBA_BLOCK
    page_spec = pl.BlockSpec((1, PAGE_SIZE, N_KV_HEADS, HEAD_DIM), lambda b, pt: (b, 0, 0, 0))
    q_spec = pl.BlockSpec((1, N_KV_HEADS, ROWS_PER_KV, HEAD_DIM), lambda b, pt: (b, 0, 0, 0))
    grid_spec = pltpu.PrefetchScalarGridSpec(
        num_scalar_prefetch=1,
        grid=(nb,),
        in_specs=[q_spec, page_spec, page_spec,
                  pl.BlockSpec(memory_space=pl.ANY), pl.BlockSpec(memory_space=pl.ANY)],
        out_specs=q_spec,
        scratch_shapes=[
            pltpu.VMEM((CHUNK_SLOTS, PAGES_PER_CHUNK * PAGE_SIZE, N_KV_HEADS, HEAD_DIM), F32),
            pltpu.SemaphoreType.DMA((CHUNK_SLOTS, PAGES_PER_CHUNK)),
            pltpu.VMEM((N_KV_HEADS, n_past, ROWS_PER_KV, MOBA_BLOCK), F32),
            pltpu.VMEM((n_past, N_KV_HEADS, HEAD_DIM), F32),
            pltpu.VMEM((N_KV_HEADS, ROWS_PER_KV, n_past), F32),
            pltpu.VMEM((N_KV_HEADS, ROWS_PER_KV, HEAD_DIM), F32),
        ],
    )
    return pl.pallas_call(
        functools.partial(_moba_sample_kernel, n_pages=n_pages),
        grid_spec=grid_spec,
        out_shape=jax.ShapeDtypeStruct(q_rows.shape, F32),
        compiler_params=_params("arbitrary"),
        name="moba_sample",
    )(page_table.reshape(-1), q_rows, k_own, v_own, pool_k, pool_v)


def sample_query_rows(q):
    nb, t, _ = q.shape
    q = q.reshape(nb, t, N_KV_HEADS, GROUP, HEAD_DIM).transpose(0, 2, 3, 1, 4)
    return q.reshape(nb, N_KV_HEADS, GROUP * t, HEAD_DIM)


def heads_from_rows(o, t):
    nb = o.shape[0]
    o = o.reshape(nb, N_KV_HEADS, GROUP, t, HEAD_DIM).transpose(0, 3, 1, 2, 4)
    return o.reshape(nb * t, N_HEADS * HEAD_DIM)


def own_page(x):
    nb, t, _ = x.shape
    x = x.reshape(nb, t, N_KV_HEADS, HEAD_DIM)
    return jnp.pad(x, ((0, 0), (0, PAGE_SIZE - t), (0, 0), (0, 0)))


def kernel(x_prompt, x_sample, cache_swa_k, cache_swa_v, cache_moba_k, cache_moba_v, page_table, attn_norm, w_qkv, w_o, swa_sinks, ffn1_norm, ffn1_w_gate, ffn1_w_up, ffn1_w_down, ffn2_norm, ffn2_w_gate, ffn2_w_up, ffn2_w_down, final_norm):
    batch, seq, d = x_prompt.shape
    dec_batch, dec_seq, _ = x_sample.shape
    assert dec_seq == DEC_SEQ
    mp = batch * seq
    depth = w_qkv.shape[0]
    q_dim = N_HEADS * HEAD_DIM
    kv_dim = N_KV_HEADS * HEAD_DIM

    x = jnp.concatenate([x_prompt.reshape(mp, d), x_sample.reshape(dec_batch * dec_seq, d)], axis=0)
    pos = jnp.concatenate([jnp.tile(jnp.arange(seq), batch), jnp.tile(PAST_LEN + jnp.arange(dec_seq), dec_batch)])
    cosf, sinf = rope_tables(pos)

    def half_ffn(x, gain, wg, wu, wd):
        pad = D_FF_PAD - D_FF
        wg = jnp.pad(wg.astype(BF16), ((0, 0), (0, pad)))
        wu = jnp.pad(wu.astype(BF16), ((0, 0), (0, pad)))
        wd = jnp.pad(wd.astype(BF16), ((0, pad), (0, 0)))
        a = gate_up(rmsnorm(x, gain, BF16), wg, wu)
        return matmul_residual(a, wd, x, 0.5, DOWN_TN, DOWN_TK)

    outs = {}
    for i in range(depth):
        a_idx = i // 2
        x = half_ffn(x, ffn1_norm[i], ffn1_w_gate[i], ffn1_w_up[i], ffn1_w_down[i])
        qkv = qkv_rope(rmsnorm(x, attn_norm[i], BF16), w_qkv[i].astype(BF16), cosf, sinf)
        qkv_s = qkv[mp:].reshape(dec_batch, dec_seq, -1)
        k_new_p = qkv[:mp, q_dim:q_dim + kv_dim].reshape(batch, seq, N_KV_HEADS, HEAD_DIM)
        v_new_p = qkv[:mp, q_dim + kv_dim:].reshape(batch, seq, N_KV_HEADS, HEAD_DIM)
        k_new_s = qkv_s[..., q_dim:q_dim + kv_dim]
        v_new_s = qkv_s[..., q_dim + kv_dim:]
        q_rows = sample_query_rows(qkv_s[..., :q_dim])
        if i % 2 == 0:
            o_p = swa_prompt(qkv, swa_sinks[a_idx], batch, seq)
            o_s, nk, nv = swa_sample(swa_sinks[a_idx], q_rows, own_page(k_new_s), own_page(v_new_s),
                                     cache_swa_k[a_idx], cache_swa_v[a_idx])
            outs.setdefault("swa_kp", []).append(k_new_p[:, -WINDOW:])
            outs.setdefault("swa_vp", []).append(v_new_p[:, -WINDOW:])
            outs.setdefault("swa_ks", []).append(nk)
            outs.setdefault("swa_vs", []).append(nv)
        else:
            o_p = moba_prompt(qkv, batch, seq)
            o_s = moba_sample(page_table, q_rows, own_page(k_new_s), own_page(v_new_s),
                              cache_moba_k[a_idx], cache_moba_v[a_idx])
            outs.setdefault("moba_kp", []).append(k_new_p)
            outs.setdefault("moba_vp", []).append(v_new_p)
            outs.setdefault("moba_ks", []).append(k_new_s.reshape(dec_batch, dec_seq, N_KV_HEADS, HEAD_DIM))
            outs.setdefault("moba_vs", []).append(v_new_s.reshape(dec_batch, dec_seq, N_KV_HEADS, HEAD_DIM))
        o = jnp.concatenate([o_p, heads_from_rows(o_s, dec_seq).astype(BF16)], axis=0)
        x = matmul_residual(o, w_o[i].astype(BF16), x, 1.0, OUT_TN, OUT_TK)
        x = half_ffn(x, ffn2_norm[i], ffn2_w_gate[i], ffn2_w_up[i], ffn2_w_down[i])

    y = rmsnorm(x, final_norm, F32)
    y_prompt = y[:mp].reshape(batch, seq, d)
    y_sample = y[mp:].reshape(dec_batch, dec_seq, d)
    return (y_prompt, y_sample,
            jnp.stack(outs["swa_kp"]), jnp.stack(outs["swa_vp"]),
            jnp.stack(outs["swa_ks"]), jnp.stack(outs["swa_vs"]),
            jnp.stack(outs["moba_kp"]), jnp.stack(outs["moba_vp"]),
            jnp.stack(outs["moba_ks"]), jnp.stack(outs["moba_vs"]))
```

```python
import functools

import jax
import jax.numpy as jnp
from jax import lax
from jax.experimental import pallas as pl
from jax.experimental.pallas import tpu as pltpu

F32 = jnp.float32
BF16 = jnp.bfloat16

D_MODEL = 4096
N_HEADS = 32
N_KV_HEADS = 8
HEAD_DIM = 128
GROUP = N_HEADS // N_KV_HEADS
KV_DIM = N_KV_HEADS * HEAD_DIM
D_FF = 11008
WINDOW = 128
MOBA_BLOCK = 256
MOBA_TOPK = 3
PAGE_SIZE = 128
PAGE_ROWS = PAGE_SIZE * N_KV_HEADS
PAST_LEN = 16384
DEC_SEQ = 4
ROPE_THETA = 10000.0
RMS_EPS = 1e-6
SCALE = HEAD_DIM ** -0.5

V7X_VMEM_BYTES = 64 * 1024 * 1024
VMEM_LIMIT = V7X_VMEM_BYTES - 8 * 1024 * 1024
LANES = 128

ROW_TILE = 1040
NORM_TILE = 416
FF_TILE = 512
D_FF_PAD = 11264
CAST_TILE = 256
DOWN_TN = 2048
DOWN_TK = 1024
QKV_TN = 1024
OUT_TN = 1024
OUT_TK = 2048
PAGES_PER_CHUNK = 4
CHUNK_SLOTS = 3
ROWS_PER_KV = GROUP * DEC_SEQ
NEG_INF = float("-inf")
NT_DIMS = (((1,), (1,)), ((), ()))
TN_DIMS = (((0,), (0,)), ((), ()))


def _params(*sem):
    return pltpu.CompilerParams(dimension_semantics=sem, vmem_limit_bytes=VMEM_LIMIT)


def _cast_kernel(w_ref, o_ref, *, rows, cols):
    r = pl.program_id(1)

    @pl.when(r * CAST_TILE < rows)
    def _():
        o_ref[0, :, :cols] = w_ref[0].astype(BF16)
        if o_ref.shape[2] > cols:
            o_ref[0, :, cols:] = jnp.zeros((CAST_TILE, o_ref.shape[2] - cols), BF16)

    @pl.when(r * CAST_TILE >= rows)
    def _():
        o_ref[...] = jnp.zeros_like(o_ref)


def cast_weights(w, rows_pad, cols_pad):
    layers, rows, cols = w.shape
    last = rows // CAST_TILE - 1
    return pl.pallas_call(
        functools.partial(_cast_kernel, rows=rows, cols=cols),
        grid=(layers, rows_pad // CAST_TILE),
        in_specs=[pl.BlockSpec((1, CAST_TILE, cols), lambda l, r: (l, jnp.minimum(r, last), 0))],
        out_specs=pl.BlockSpec((1, CAST_TILE, cols_pad), lambda l, r: (l, r, 0)),
        out_shape=jax.ShapeDtypeStruct((layers, rows_pad, cols_pad), BF16),
        compiler_params=_params("parallel", "arbitrary"),
        name="cast_weights",
    )(w)


def _rms_kernel(x_ref, g_ref, o_ref):
    x = x_ref[...]
    ms = jnp.mean(x * x, axis=-1, keepdims=True)
    o_ref[...] = ((x * lax.rsqrt(ms + RMS_EPS)) * g_ref[...]).astype(o_ref.dtype)


def rmsnorm(x, gain, out_dtype):
    m, d = x.shape
    return pl.pallas_call(
        _rms_kernel,
        grid=(m // NORM_TILE,),
        in_specs=[pl.BlockSpec((NORM_TILE, d), lambda i: (i, 0)),
                  pl.BlockSpec((1, d), lambda i: (0, 0))],
        out_specs=pl.BlockSpec((NORM_TILE, d), lambda i: (i, 0)),
        out_shape=jax.ShapeDtypeStruct((m, d), out_dtype),
        compiler_params=_params("parallel"),
        name="rmsnorm",
    )(x, gain.reshape(1, d))


def _gateup_kernel(h_ref, wg_ref, wu_ref, o_ref):
    h = h_ref[...]
    g = jnp.dot(h, wg_ref[...], preferred_element_type=F32)
    u = jnp.dot(h, wu_ref[...], preferred_element_type=F32)
    o_ref[...] = (g * jax.nn.sigmoid(g) * u).astype(o_ref.dtype)


def gate_up(h, wg, wu, layer):
    m, d = h.shape
    f = wg.shape[2]
    w_spec = pl.BlockSpec((None, d, FF_TILE), lambda i, j: (layer, 0, j))
    return pl.pallas_call(
        _gateup_kernel,
        grid=(m // ROW_TILE, f // FF_TILE),
        in_specs=[pl.BlockSpec((ROW_TILE, d), lambda i, j: (i, 0)), w_spec, w_spec],
        out_specs=pl.BlockSpec((ROW_TILE, FF_TILE), lambda i, j: (i, j)),
        out_shape=jax.ShapeDtypeStruct((m, f), BF16),
        compiler_params=_params("parallel", "arbitrary"),
        name="gate_up",
    )(h, wg, wu)


def _matmul_residual_kernel(a_ref, w_ref, r_ref, o_ref, *, scale):
    k = pl.program_id(2)
    p = scale * jnp.dot(a_ref[...], w_ref[...], preferred_element_type=F32)

    @pl.when(k == 0)
    def _():
        o_ref[...] = r_ref[...] + p

    @pl.when(k > 0)
    def _():
        o_ref[...] += p


def matmul_residual(a, w, layer, res, scale, tn, tk):
    m, kdim = a.shape
    n = w.shape[2]
    return pl.pallas_call(
        functools.partial(_matmul_residual_kernel, scale=scale),
        grid=(m // ROW_TILE, n // tn, kdim // tk),
        in_specs=[pl.BlockSpec((ROW_TILE, tk), lambda i, j, k: (i, k)),
                  pl.BlockSpec((None, tk, tn), lambda i, j, k: (layer, k, j)),
                  pl.BlockSpec((ROW_TILE, tn), lambda i, j, k: (i, j))],
        out_specs=pl.BlockSpec((ROW_TILE, tn), lambda i, j, k: (i, j)),
        out_shape=jax.ShapeDtypeStruct((m, n), F32),
        compiler_params=_params("parallel", "parallel", "arbitrary"),
        name="matmul_residual",
    )(a, w, res)


def _qkv_kernel(h_ref, w_ref, cos_ref, sin_ref, o_ref, *, rope_tiles):
    acc = jnp.dot(h_ref[...], w_ref[...], preferred_element_type=F32)
    n = pl.program_id(1)

    @pl.when(n < rope_tiles)
    def _():
        cosf = cos_ref[...]
        sinf = sin_ref[...]
        for c in range(acc.shape[1] // HEAD_DIM):
            x = acc[:, c * HEAD_DIM:(c + 1) * HEAD_DIM]
            o_ref[:, c * HEAD_DIM:(c + 1) * HEAD_DIM] = x * cosf + pltpu.roll(x, HEAD_DIM // 2, 1) * sinf

    @pl.when(n >= rope_tiles)
    def _():
        o_ref[...] = acc


def qkv_rope(h, w, layer, cosf, sinf):
    m, d = h.shape
    n = w.shape[2]
    rope_tiles = (N_HEADS + N_KV_HEADS) * HEAD_DIM // QKV_TN
    return pl.pallas_call(
        functools.partial(_qkv_kernel, rope_tiles=rope_tiles),
        grid=(m // ROW_TILE, n // QKV_TN),
        in_specs=[pl.BlockSpec((ROW_TILE, d), lambda i, j: (i, 0)),
                  pl.BlockSpec((None, d, QKV_TN), lambda i, j: (layer, 0, j)),
                  pl.BlockSpec((ROW_TILE, HEAD_DIM), lambda i, j: (i, 0)),
                  pl.BlockSpec((ROW_TILE, HEAD_DIM), lambda i, j: (i, 0))],
        out_specs=pl.BlockSpec((ROW_TILE, QKV_TN), lambda i, j: (i, j)),
        out_shape=jax.ShapeDtypeStruct((m, n), F32),
        compiler_params=_params("parallel", "arbitrary"),
        name="qkv_rope",
    )(h, w, cosf, sinf)


def rope_tables(pos):
    inv = ROPE_THETA ** (-jnp.arange(0, HEAD_DIM, 2, dtype=F32) / HEAD_DIM)
    ang = pos.astype(F32)[:, None] * inv[None, :]
    cos = jnp.cos(ang)
    sin = jnp.sin(ang)
    return jnp.concatenate([cos, cos], axis=-1), jnp.concatenate([-sin, sin], axis=-1)


def _swa_prompt_kernel(sink_ref, q_ref, kp_ref, kc_ref, vp_ref, vc_ref, o_ref):
    g = pl.program_id(1)
    n = pl.program_id(2)
    k = jnp.concatenate([kp_ref[...], kc_ref[...]], axis=0).astype(BF16)
    v = jnp.concatenate([vp_ref[...], vc_ref[...]], axis=0).astype(BF16)
    r = lax.broadcasted_iota(jnp.int32, (WINDOW, 2 * WINDOW), 0)
    c = lax.broadcasted_iota(jnp.int32, (WINDOW, 2 * WINDOW), 1)
    mask = (c > r) & (c <= r + WINDOW) & ((c >= WINDOW) | (n > 0))
    for j in range(GROUP):
        q = q_ref[:, j * HEAD_DIM:(j + 1) * HEAD_DIM].astype(BF16)
        s = lax.dot_general(q, k, NT_DIMS, preferred_element_type=F32) * SCALE
        s = jnp.where(mask, s, NEG_INF)
        sink = sink_ref[g * GROUP + j]
        m = jnp.maximum(jnp.max(s, axis=-1, keepdims=True), sink)
        p = jnp.exp(s - m)
        denom = jnp.sum(p, axis=-1, keepdims=True) + jnp.exp(sink - m)
        o = jnp.dot(p.astype(BF16), v, preferred_element_type=F32) / denom
        o_ref[:, j * HEAD_DIM:(j + 1) * HEAD_DIM] = o.astype(o_ref.dtype)


def swa_prompt(qkv, sinks, batch, seq):
    nb = seq // WINDOW
    kcol = N_HEADS
    vcol = N_HEADS + N_KV_HEADS
    qw = GROUP * HEAD_DIM
    return pl.pallas_call(
        _swa_prompt_kernel,
        grid=(batch, N_KV_HEADS, nb),
        in_specs=[
            pl.BlockSpec(memory_space=pltpu.SMEM),
            pl.BlockSpec((WINDOW, qw), lambda b, g, n: (b * nb + n, g)),
            pl.BlockSpec((WINDOW, HEAD_DIM), lambda b, g, n: (b * nb + jnp.maximum(n - 1, 0), kcol + g)),
            pl.BlockSpec((WINDOW, HEAD_DIM), lambda b, g, n: (b * nb + n, kcol + g)),
            pl.BlockSpec((WINDOW, HEAD_DIM), lambda b, g, n: (b * nb + jnp.maximum(n - 1, 0), vcol + g)),
            pl.BlockSpec((WINDOW, HEAD_DIM), lambda b, g, n: (b * nb + n, vcol + g)),
        ],
        out_specs=pl.BlockSpec((WINDOW, qw), lambda b, g, n: (b * nb + n, g)),
        out_shape=jax.ShapeDtypeStruct((batch * seq, N_HEADS * HEAD_DIM), BF16),
        compiler_params=_params("parallel", "parallel", "arbitrary"),
        name="swa_prompt",
    )(sinks, qkv, qkv, qkv, qkv, qkv)


def _moba_prompt_kernel(q_ref, k_ref, v_ref, o_ref, means_ref, sel_ref, acc_ref, *, nblk):
    c = pl.program_id(2)
    nq = GROUP * MOBA_BLOCK

    @pl.when(c == 0)
    def _():
        for n in range(nblk):
            kb = k_ref[n * MOBA_BLOCK:(n + 1) * MOBA_BLOCK, :]
            means_ref[n:n + 1, :] = jnp.sum(kb, axis=0, keepdims=True) / MOBA_BLOCK

    qf = jnp.concatenate([q_ref[:, j * HEAD_DIM:(j + 1) * HEAD_DIM] for j in range(GROUP)], axis=0)
    q = qf.astype(BF16)

    gate = lax.dot_general(means_ref[...], qf, NT_DIMS,
                           precision=lax.Precision.HIGHEST, preferred_element_type=F32)
    blk = lax.broadcasted_iota(jnp.int32, gate.shape, 0)
    gate = jnp.where(blk < c, gate, NEG_INF)
    for n in range(nblk):
        gate_n = gate[n:n + 1, :]
        ahead = (gate > gate_n) | ((gate == gate_n) & (blk < n))
        rank = jnp.sum(jnp.where(ahead, 1.0, 0.0), axis=0, keepdims=True)
        sel_ref[n:n + 1, :] = jnp.where(rank < MOBA_TOPK, 1.0, 0.0)

    def block_scores(off):
        kb = k_ref[pl.ds(off, MOBA_BLOCK), :].astype(BF16)
        return lax.dot_general(kb, q, NT_DIMS, preferred_element_type=F32) * SCALE

    def weighted_values(off, p):
        vb = v_ref[pl.ds(off, MOBA_BLOCK), :].astype(BF16)
        return lax.dot_general(vb, p.astype(BF16), TN_DIMS, preferred_element_type=F32)

    own = pl.multiple_of(c * MOBA_BLOCK, MOBA_BLOCK)
    s = block_scores(own)
    key = lax.broadcasted_iota(jnp.int32, s.shape, 0)
    query = lax.broadcasted_iota(jnp.int32, s.shape, 1) % MOBA_BLOCK
    s = jnp.where(key <= query, s, NEG_INF)
    m0 = jnp.max(s, axis=0, keepdims=True)
    p = jnp.exp(s - m0)
    l0 = jnp.sum(p, axis=0, keepdims=True)
    acc_ref[...] = weighted_values(own, p)

    def past_block(n, carry):
        m, l = carry
        off = pl.multiple_of(n * MOBA_BLOCK, MOBA_BLOCK)
        sb = jnp.where(sel_ref[pl.ds(n, 1), :] > 0.0, block_scores(off), NEG_INF)
        m_new = jnp.maximum(m, jnp.max(sb, axis=0, keepdims=True))
        alpha = jnp.exp(m - m_new)
        pb = jnp.exp(sb - m_new)
        acc_ref[...] = alpha * acc_ref[...] + weighted_values(off, pb)
        return m_new, alpha * l + jnp.sum(pb, axis=0, keepdims=True)

    _, l = lax.fori_loop(0, c, past_block, (m0, l0))
    out = acc_ref[...] / l
    for j in range(GROUP):
        o_ref[:, j * HEAD_DIM:(j + 1) * HEAD_DIM] = out[:, j * MOBA_BLOCK:(j + 1) * MOBA_BLOCK].T.astype(o_ref.dtype)


def moba_prompt(qkv, batch, seq):
    nblk = seq // MOBA_BLOCK
    kcol = N_HEADS
    vcol = N_HEADS + N_KV_HEADS
    qw = GROUP * HEAD_DIM
    nq = GROUP * MOBA_BLOCK
    return pl.pallas_call(
        functools.partial(_moba_prompt_kernel, nblk=nblk),
        grid=(batch, N_KV_HEADS, nblk),
        in_specs=[
            pl.BlockSpec((MOBA_BLOCK, qw), lambda b, g, c: (b * nblk + c, g)),
            pl.BlockSpec((seq, HEAD_DIM), lambda b, g, c: (b, kcol + g)),
            pl.BlockSpec((seq, HEAD_DIM), lambda b, g, c: (b, vcol + g)),
        ],
        out_specs=pl.BlockSpec((MOBA_BLOCK, qw), lambda b, g, c: (b * nblk + c, g)),
        out_shape=jax.ShapeDtypeStruct((batch * seq, N_HEADS * HEAD_DIM), BF16),
        scratch_shapes=[pltpu.VMEM((nblk, HEAD_DIM), F32),
                        pltpu.VMEM((nblk, nq), F32),
                        pltpu.VMEM((HEAD_DIM, nq), F32)],
        compiler_params=_params("parallel", "parallel", "arbitrary"),
        name="moba_prompt",
    )(qkv, qkv, qkv)


def _swa_sample_kernel(sink_ref, q_ref, kn_ref, vn_ref, ck_ref, cv_ref, o_ref, nk_ref, nv_ref):
    keep = WINDOW - DEC_SEQ
    nk_ref[0, 0:keep] = ck_ref[0, DEC_SEQ:WINDOW]
    nk_ref[0, keep:WINDOW] = kn_ref[0, 0:DEC_SEQ]
    nv_ref[0, 0:keep] = cv_ref[0, DEC_SEQ:WINDOW]
    nv_ref[0, keep:WINDOW] = vn_ref[0, 0:DEC_SEQ]

    t = lax.broadcasted_iota(jnp.int32, (ROWS_PER_KV, 2 * WINDOW), 0) % DEC_SEQ
    col = lax.broadcasted_iota(jnp.int32, (ROWS_PER_KV, 2 * WINDOW), 1)
    mask = ((col < WINDOW) & (col > t)) | ((col >= WINDOW) & (col - WINDOW <= t))
    row = lax.broadcasted_iota(jnp.int32, (ROWS_PER_KV, 1), 0)
    for g in range(N_KV_HEADS):
        q = q_ref[0, g].astype(BF16)
        k = jnp.concatenate([ck_ref[0, :, g, :], kn_ref[0, :, g, :]], axis=0).astype(BF16)
        v = jnp.concatenate([cv_ref[0, :, g, :], vn_ref[0, :, g, :]], axis=0).astype(BF16)
        s = lax.dot_general(q, k, NT_DIMS, preferred_element_type=F32) * SCALE
        s = jnp.where(mask, s, NEG_INF)
        sink = jnp.zeros((ROWS_PER_KV, 1), F32)
        for j in range(GROUP):
            sink = jnp.where(row // DEC_SEQ == j, sink_ref[g * GROUP + j], sink)
        m = jnp.maximum(jnp.max(s, axis=-1, keepdims=True), sink)
        p = jnp.exp(s - m)
        denom = jnp.sum(p, axis=-1, keepdims=True) + jnp.exp(sink - m)
        o_ref[0, g] = jnp.dot(p.astype(BF16), v, preferred_element_type=F32) / denom


def swa_sample(sinks, q_rows, k_own, v_own, cache_k, cache_v):
    nb = q_rows.shape[0]
    page_spec = pl.BlockSpec((1, PAGE_SIZE, N_KV_HEADS, HEAD_DIM), lambda b: (b, 0, 0, 0))
    q_spec = pl.BlockSpec((1, N_KV_HEADS, ROWS_PER_KV, HEAD_DIM), lambda b: (b, 0, 0, 0))
    cache_shape = jax.ShapeDtypeStruct((nb, WINDOW, N_KV_HEADS, HEAD_DIM), F32)
    return pl.pallas_call(
        _swa_sample_kernel,
        grid=(nb,),
        in_specs=[pl.BlockSpec(memory_space=pltpu.SMEM), q_spec, page_spec, page_spec, page_spec, page_spec],
        out_specs=[q_spec, page_spec, page_spec],
        out_shape=[jax.ShapeDtypeStruct(q_rows.shape, F32), cache_shape, cache_shape],
        compiler_params=_params("parallel"),
        name="swa_sample",
    )(sinks, q_rows, k_own, v_own, cache_k, cache_v)


def _head_concat(flat_ref, first_token, n_tokens):
    return jnp.concatenate(
        [flat_ref[pl.ds(first_token * N_KV_HEADS + g, n_tokens, stride=N_KV_HEADS), :]
         for g in range(N_KV_HEADS)], axis=1)


def _lane_to_rows(row_vec):
    i = lax.broadcasted_iota(jnp.int32, (LANES, LANES), 0)
    j = lax.broadcasted_iota(jnp.int32, (LANES, LANES), 1)
    return jnp.sum(jnp.where(i == j, row_vec, 0.0), axis=1, keepdims=True)


def _moba_sample_kernel(pt_ref, qbd_ref, kn_ref, vn_ref, kpool, vpool, o_ref,
                        buf, sem, scores, means, sel, acc, *, n_pages):
    b = pl.program_id(0)
    chunk_tokens = PAGES_PER_CHUNK * PAGE_SIZE
    n_chunks = n_pages // PAGES_PER_CHUNK
    blocks_per_chunk = chunk_tokens // MOBA_BLOCK
    n_past = n_pages * PAGE_SIZE // MOBA_BLOCK

    def chunk_copy(pool, ci, p):
        page = pt_ref[b * n_pages + ci * PAGES_PER_CHUNK + p]
        slot = ci % CHUNK_SLOTS
        return pltpu.make_async_copy(pool.at[page], buf.at[slot, pl.ds(p * PAGE_ROWS, PAGE_ROWS)],
                                     sem.at[slot, p])

    def start_chunk(pool, ci):
        for p in range(PAGES_PER_CHUNK):
            chunk_copy(pool, ci, p).start()

    def wait_chunk(pool, ci):
        for p in range(PAGES_PER_CHUNK):
            chunk_copy(pool, ci, p).wait()

    def stream(pool, per_block):
        for ci in range(CHUNK_SLOTS - 1):
            start_chunk(pool, ci)

        def body(ci, carry):
            @pl.when(ci + CHUNK_SLOTS - 1 < n_chunks)
            def _():
                start_chunk(pool, ci + CHUNK_SLOTS - 1)

            wait_chunk(pool, ci)
            flat = buf.at[ci % CHUNK_SLOTS]
            for h in range(blocks_per_chunk):
                per_block(ci * blocks_per_chunk + h, _head_concat(flat, h * MOBA_BLOCK, MOBA_BLOCK))
            return carry

        lax.fori_loop(0, n_chunks, body, 0)

    def scores_of(rows, qbd):
        return jnp.dot(rows.astype(BF16), qbd, preferred_element_type=F32) * SCALE

    def weighted_values(p, rows):
        return lax.dot_general(p.astype(BF16), rows.astype(BF16), TN_DIMS, preferred_element_type=F32)

    qbd_f32 = qbd_ref[0]
    qbd = qbd_f32.astype(BF16)

    def k_block(n, rows):
        means[pl.ds(n, 1), :] = jnp.sum(rows, axis=0, keepdims=True) / MOBA_BLOCK
        off = pl.multiple_of(n * MOBA_BLOCK, MOBA_BLOCK)
        scores[pl.ds(off, MOBA_BLOCK), :] = scores_of(rows, qbd)

    stream(kpool, k_block)

    gate = jnp.dot(means[...], qbd_f32, precision=lax.Precision.HIGHEST, preferred_element_type=F32)
    blk = lax.broadcasted_iota(jnp.int32, gate.shape, 0).astype(F32)
    chosen = jnp.zeros(gate.shape, F32)
    for _ in range(MOBA_TOPK):
        best = jnp.max(gate, axis=0, keepdims=True)
        first = jnp.min(jnp.where(gate == best, blk, float(n_past)), axis=0, keepdims=True)
        hit = blk == first
        chosen = jnp.where(hit, 1.0, chosen)
        gate = jnp.where(hit, NEG_INF, gate)
    sel[...] = chosen

    own_k = _head_concat(kn_ref.at[0], 0, PAGE_SIZE)
    own_v = _head_concat(vn_ref.at[0], 0, PAGE_SIZE)
    s_own = scores_of(own_k, qbd)
    row = lax.broadcasted_iota(jnp.int32, s_own.shape, 0)
    t = lax.broadcasted_iota(jnp.int32, s_own.shape, 1) % DEC_SEQ
    s_own = jnp.where(row <= t, s_own, NEG_INF)

    def block_scores(n):
        off = pl.multiple_of(n * MOBA_BLOCK, MOBA_BLOCK)
        return off, jnp.where(sel[pl.ds(n, 1), :] > 0.0, scores[pl.ds(off, MOBA_BLOCK), :], NEG_INF)

    def max_body(n, m):
        return jnp.maximum(m, jnp.max(block_scores(n)[1], axis=0, keepdims=True))

    m = lax.fori_loop(0, n_past, max_body, jnp.max(s_own, axis=0, keepdims=True))
    p_own = jnp.exp(s_own - m)

    def exp_body(n, l):
        off, s = block_scores(n)
        p = jnp.exp(s - m)
        scores[pl.ds(off, MOBA_BLOCK), :] = p
        return l + jnp.sum(p, axis=0, keepdims=True)

    l = lax.fori_loop(0, n_past, exp_body, jnp.sum(p_own, axis=0, keepdims=True))

    acc[...] = weighted_values(p_own, own_v)

    def v_block(n, rows):
        off = pl.multiple_of(n * MOBA_BLOCK, MOBA_BLOCK)
        acc[...] += weighted_values(scores[pl.ds(off, MOBA_BLOCK), :], rows)

    stream(vpool, v_block)

    inv = 1.0 / _lane_to_rows(l)
    for g in range(N_KV_HEADS):
        rows = slice(g * ROWS_PER_KV, (g + 1) * ROWS_PER_KV)
        o_ref[0, g] = acc[rows, g * HEAD_DIM:(g + 1) * HEAD_DIM] * inv[rows]


def moba_sample(page_table, qbd, k_own, v_own, pool_k, pool_v):
    nb, n_pages = page_table.shape
    n_past = n_pages * PAGE_SIZE // MOBA_BLOCK
    page_spec = pl.BlockSpec((1, PAGE_ROWS, HEAD_DIM), lambda b, pt: (b, 0, 0))
    grid_spec = pltpu.PrefetchScalarGridSpec(
        num_scalar_prefetch=1,
        grid=(nb,),
        in_specs=[pl.BlockSpec((1, KV_DIM, LANES), lambda b, pt: (b, 0, 0)), page_spec, page_spec,
                  pl.BlockSpec(memory_space=pl.ANY), pl.BlockSpec(memory_space=pl.ANY)],
        out_specs=pl.BlockSpec((1, N_KV_HEADS, ROWS_PER_KV, HEAD_DIM), lambda b, pt: (b, 0, 0, 0)),
        scratch_shapes=[
            pltpu.VMEM((CHUNK_SLOTS, PAGES_PER_CHUNK * PAGE_ROWS, HEAD_DIM), F32),
            pltpu.SemaphoreType.DMA((CHUNK_SLOTS, PAGES_PER_CHUNK)),
            pltpu.VMEM((n_past * MOBA_BLOCK, LANES), F32),
            pltpu.VMEM((n_past, KV_DIM), F32),
            pltpu.VMEM((n_past, LANES), F32),
            pltpu.VMEM((LANES, KV_DIM), F32),
        ],
    )
    return pl.pallas_call(
        functools.partial(_moba_sample_kernel, n_pages=n_pages),
        grid_spec=grid_spec,
        out_shape=jax.ShapeDtypeStruct((nb, N_KV_HEADS, ROWS_PER_KV, HEAD_DIM), F32),
        compiler_params=_params("arbitrary"),
        name="moba_sample",
    )(page_table.reshape(-1), qbd, k_own, v_own, pool_k, pool_v)


def sample_query_rows(q):
    nb, t, _ = q.shape
    q = q.reshape(nb, t, N_KV_HEADS, GROUP, HEAD_DIM).transpose(0, 2, 3, 1, 4)
    return q.reshape(nb, N_KV_HEADS, GROUP * t, HEAD_DIM)


def block_diag_queries(q_rows):
    nb = q_rows.shape[0]
    eye = jnp.eye(N_KV_HEADS, dtype=q_rows.dtype)
    return jnp.einsum("bgrd,gh->bgdhr", q_rows, eye).reshape(nb, KV_DIM, N_KV_HEADS * ROWS_PER_KV)


def heads_from_rows(o, t):
    nb = o.shape[0]
    o = o.reshape(nb, N_KV_HEADS, GROUP, t, HEAD_DIM).transpose(0, 3, 1, 2, 4)
    return o.reshape(nb * t, N_HEADS * HEAD_DIM)


def own_page(x):
    nb, t, _ = x.shape
    x = x.reshape(nb, t, N_KV_HEADS, HEAD_DIM)
    return jnp.pad(x, ((0, 0), (0, PAGE_SIZE - t), (0, 0), (0, 0)))


def kernel(x_prompt, x_sample, cache_swa_k, cache_swa_v, cache_moba_k, cache_moba_v, page_table, attn_norm, w_qkv, w_o, swa_sinks, ffn1_norm, ffn1_w_gate, ffn1_w_up, ffn1_w_down, ffn2_norm, ffn2_w_gate, ffn2_w_up, ffn2_w_down, final_norm):
    batch, seq, d = x_prompt.shape
    dec_batch, dec_seq, _ = x_sample.shape
    assert dec_seq == DEC_SEQ
    mp = batch * seq
    depth = w_qkv.shape[0]
    q_dim = N_HEADS * HEAD_DIM

    x = jnp.concatenate([x_prompt.reshape(mp, d), x_sample.reshape(dec_batch * dec_seq, d)], axis=0)
    pos = jnp.concatenate([jnp.tile(jnp.arange(seq), batch), jnp.tile(PAST_LEN + jnp.arange(dec_seq), dec_batch)])
    cosf, sinf = rope_tables(pos)

    ffn = []
    for norm, wg, wu, wd in ((ffn1_norm, ffn1_w_gate, ffn1_w_up, ffn1_w_down),
                             (ffn2_norm, ffn2_w_gate, ffn2_w_up, ffn2_w_down)):
        ffn.append((norm, cast_weights(wg, d, D_FF_PAD), cast_weights(wu, d, D_FF_PAD),
                    cast_weights(wd, D_FF_PAD, d)))
    w_qkv_b = cast_weights(w_qkv, d, w_qkv.shape[2])
    w_o_b = cast_weights(w_o, q_dim, d)

    def half_ffn(x, which, layer):
        norm, wg, wu, wd = ffn[which]
        a = gate_up(rmsnorm(x, norm[layer], BF16), wg, wu, layer)
        return matmul_residual(a, wd, layer, x, 0.5, DOWN_TN, DOWN_TK)

    outs = {}
    for i in range(depth):
        a_idx = i // 2
        x = half_ffn(x, 0, i)
        qkv = qkv_rope(rmsnorm(x, attn_norm[i], BF16), w_qkv_b, i, cosf, sinf)
        qkv_s = qkv[mp:].reshape(dec_batch, dec_seq, -1)
        k_new_p = qkv[:mp, q_dim:q_dim + KV_DIM].reshape(batch, seq, N_KV_HEADS, HEAD_DIM)
        v_new_p = qkv[:mp, q_dim + KV_DIM:].reshape(batch, seq, N_KV_HEADS, HEAD_DIM)
        k_new_s = qkv_s[..., q_dim:q_dim + KV_DIM]
        v_new_s = qkv_s[..., q_dim + KV_DIM:]
        q_rows = sample_query_rows(qkv_s[..., :q_dim])
        if i % 2 == 0:
            o_p = swa_prompt(qkv, swa_sinks[a_idx], batch, seq)
            o_s, nk, nv = swa_sample(swa_sinks[a_idx], q_rows, own_page(k_new_s), own_page(v_new_s),
                                     cache_swa_k[a_idx], cache_swa_v[a_idx])
            outs.setdefault("swa_kp", []).append(k_new_p[:, -WINDOW:])
            outs.setdefault("swa_vp", []).append(v_new_p[:, -WINDOW:])
            outs.setdefault("swa_ks", []).append(nk)
            outs.setdefault("swa_vs", []).append(nv)
        else:
            o_p = moba_prompt(qkv, batch, seq)
            flat = (-1, PAGE_ROWS, HEAD_DIM)
            o_s = moba_sample(page_table, block_diag_queries(q_rows),
                              own_page(k_new_s).reshape(flat), own_page(v_new_s).reshape(flat),
                              cache_moba_k[a_idx].reshape(flat), cache_moba_v[a_idx].reshape(flat))
            outs.setdefault("moba_kp", []).append(k_new_p)
            outs.setdefault("moba_vp", []).append(v_new_p)
            outs.setdefault("moba_ks", []).append(k_new_s.reshape(dec_batch, dec_seq, N_KV_HEADS, HEAD_DIM))
            outs.setdefault("moba_vs", []).append(v_new_s.reshape(dec_batch, dec_seq, N_KV_HEADS, HEAD_DIM))
        o = jnp.concatenate([o_p, heads_from_rows(o_s, dec_seq).astype(BF16)], axis=0)
        x = matmul_residual(o, w_o_b, i, x, 1.0, OUT_TN, OUT_TK)
        x = half_ffn(x, 1, i)

    y = rmsnorm(x, final_norm, F32)
    y_prompt = y[:mp].reshape(batch, seq, d)
    y_sample = y[mp:].reshape(dec_batch, dec_seq, d)
    return (y_prompt, y_sample,
            jnp.stack(outs["swa_kp"]), jnp.stack(outs["swa_vp"]),
            jnp.stack(outs["swa_ks"]), jnp.stack(outs["swa_vs"]),
            jnp.stack(outs["moba_kp"]), jnp.stack(outs["moba_vp"]),
            jnp.stack(outs["moba_ks"]), jnp.stack(outs["moba_vs"]))
```

```python
import functools

import jax
import jax.numpy as jnp
from jax import lax
from jax.experimental import pallas as pl
from jax.experimental.pallas import tpu as pltpu

F32 = jnp.float32
BF16 = jnp.bfloat16

D_MODEL = 4096
N_HEADS = 32
N_KV_HEADS = 8
HEAD_DIM = 128
GROUP = N_HEADS // N_KV_HEADS
KV_DIM = N_KV_HEADS * HEAD_DIM
D_FF = 11008
WINDOW = 128
MOBA_BLOCK = 256
MOBA_TOPK = 3
PAGE_SIZE = 128
PAGE_ROWS = PAGE_SIZE * N_KV_HEADS
PAST_LEN = 16384
DEC_SEQ = 4
ROPE_THETA = 10000.0
RMS_EPS = 1e-6
SCALE = HEAD_DIM ** -0.5

V7X_VMEM_BYTES = 64 * 1024 * 1024
VMEM_LIMIT = V7X_VMEM_BYTES - 8 * 1024 * 1024
LANES = 128

ROW_TILE = 1040
NORM_TILE = 416
FINAL_TILE = 512
SWA_QT = 256
GATE_ROW_TILE = 2080
FF_TILE = 256
D_FF_PAD = 11264
CAST_TILE = 256
DOWN_TN = 1024
DOWN_TK = 2816
QKV_TN = 1024
OUT_TN = 1024
OUT_TK = 2048
PAGES_PER_CHUNK = 4
CHUNK_SLOTS = 3
ROWS_PER_KV = GROUP * DEC_SEQ
NEG_INF = float("-inf")
NT_DIMS = (((1,), (1,)), ((), ()))
TN_DIMS = (((0,), (0,)), ((), ()))


def _params(*sem):
    return pltpu.CompilerParams(dimension_semantics=sem, vmem_limit_bytes=VMEM_LIMIT)


def _cast_kernel(w_ref, o_ref, *, rows, cols):
    r = pl.program_id(1)

    @pl.when(r * CAST_TILE < rows)
    def _():
        o_ref[0, :, :cols] = w_ref[0].astype(BF16)
        if o_ref.shape[2] > cols:
            o_ref[0, :, cols:] = jnp.zeros((CAST_TILE, o_ref.shape[2] - cols), BF16)

    @pl.when(r * CAST_TILE >= rows)
    def _():
        o_ref[...] = jnp.zeros_like(o_ref)


def cast_weights(w, rows_pad, cols_pad):
    layers, rows, cols = w.shape
    last = rows // CAST_TILE - 1
    return pl.pallas_call(
        functools.partial(_cast_kernel, rows=rows, cols=cols),
        grid=(layers, rows_pad // CAST_TILE),
        in_specs=[pl.BlockSpec((1, CAST_TILE, cols), lambda l, r: (l, jnp.minimum(r, last), 0))],
        out_specs=pl.BlockSpec((1, CAST_TILE, cols_pad), lambda l, r: (l, r, 0)),
        out_shape=jax.ShapeDtypeStruct((layers, rows_pad, cols_pad), BF16),
        compiler_params=_params("parallel", "arbitrary"),
        name="cast_weights",
    )(w)


def _rms_kernel(x_ref, g_ref, o_ref):
    x = x_ref[...]
    ms = jnp.mean(x * x, axis=-1, keepdims=True)
    o_ref[...] = ((x * lax.rsqrt(ms + RMS_EPS)) * g_ref[...]).astype(o_ref.dtype)


def rmsnorm(x, gain, out_dtype, tile=NORM_TILE, first_tile=0, rows=None):
    d = x.shape[1]
    rows = x.shape[0] if rows is None else rows
    return pl.pallas_call(
        _rms_kernel,
        grid=(rows // tile,),
        in_specs=[pl.BlockSpec((tile, d), lambda i: (first_tile + i, 0)),
                  pl.BlockSpec((1, d), lambda i: (0, 0))],
        out_specs=pl.BlockSpec((tile, d), lambda i: (i, 0)),
        out_shape=jax.ShapeDtypeStruct((rows, d), out_dtype),
        compiler_params=_params("parallel"),
        name="rmsnorm",
    )(x, gain.reshape(1, d))


def _gateup_kernel(h_ref, wg_ref, wu_ref, o_ref, *, real_tiles):
    j = pl.program_id(1)

    @pl.when(j < real_tiles)
    def _():
        h = h_ref[...]
        g = jnp.dot(h, wg_ref[...].astype(BF16), preferred_element_type=F32)
        u = jnp.dot(h, wu_ref[...].astype(BF16), preferred_element_type=F32)
        o_ref[...] = (0.5 * g * jax.nn.sigmoid(g) * u).astype(o_ref.dtype)

    @pl.when(j >= real_tiles)
    def _():
        o_ref[...] = jnp.zeros_like(o_ref)


def gate_up(h, wg, wu, layer):
    m, d = h.shape
    real_tiles = wg.shape[2] // FF_TILE
    w_spec = pl.BlockSpec((None, d, FF_TILE), lambda i, j: (layer, 0, jnp.minimum(j, real_tiles - 1)))
    return pl.pallas_call(
        functools.partial(_gateup_kernel, real_tiles=real_tiles),
        grid=(m // GATE_ROW_TILE, D_FF_PAD // FF_TILE),
        in_specs=[pl.BlockSpec((GATE_ROW_TILE, d), lambda i, j: (i, 0), pipeline_mode=pl.Buffered(1)),
                  w_spec, w_spec],
        out_specs=pl.BlockSpec((GATE_ROW_TILE, FF_TILE), lambda i, j: (i, j)),
        out_shape=jax.ShapeDtypeStruct((m, D_FF_PAD), BF16),
        compiler_params=_params("parallel", "arbitrary"),
        name="gate_up",
    )(h, wg, wu)


def _matmul_residual_kernel(a_ref, w_ref, r_ref, o_ref):
    k = pl.program_id(2)

    @pl.when(k == 0)
    def _():
        o_ref[...] = r_ref[...] + jnp.dot(a_ref[...], w_ref[...], preferred_element_type=F32)

    @pl.when(k > 0)
    def _():
        o_ref[...] += jnp.dot(a_ref[...], w_ref[...], preferred_element_type=F32)


def matmul_residual(a, w, layer, res, tn, tk):
    m, kdim = a.shape
    n = w.shape[2]
    return pl.pallas_call(
        _matmul_residual_kernel,
        grid=(m // ROW_TILE, n // tn, kdim // tk),
        in_specs=[pl.BlockSpec((ROW_TILE, tk), lambda i, j, k: (i, k)),
                  pl.BlockSpec((None, tk, tn), lambda i, j, k: (layer, k, j)),
                  pl.BlockSpec((ROW_TILE, tn), lambda i, j, k: (i, j))],
        out_specs=pl.BlockSpec((ROW_TILE, tn), lambda i, j, k: (i, j)),
        out_shape=jax.ShapeDtypeStruct((m, n), F32),
        compiler_params=_params("parallel", "parallel", "arbitrary"),
        name="matmul_residual",
    )(a, w, res)


def _qkv_kernel(h_ref, w_ref, cos_ref, sin_ref, o_ref, *, rope_tiles):
    acc = jnp.dot(h_ref[...], w_ref[...], preferred_element_type=F32)
    n = pl.program_id(1)

    @pl.when(n < rope_tiles)
    def _():
        cosf = cos_ref[...]
        sinf = sin_ref[...]
        for c in range(acc.shape[1] // HEAD_DIM):
            x = acc[:, c * HEAD_DIM:(c + 1) * HEAD_DIM]
            o_ref[:, c * HEAD_DIM:(c + 1) * HEAD_DIM] = x * cosf + pltpu.roll(x, HEAD_DIM // 2, 1) * sinf

    @pl.when(n >= rope_tiles)
    def _():
        o_ref[...] = acc


def qkv_rope(h, w, layer, cosf, sinf):
    m, d = h.shape
    n = w.shape[2]
    rope_tiles = (N_HEADS + N_KV_HEADS) * HEAD_DIM // QKV_TN
    return pl.pallas_call(
        functools.partial(_qkv_kernel, rope_tiles=rope_tiles),
        grid=(m // ROW_TILE, n // QKV_TN),
        in_specs=[pl.BlockSpec((ROW_TILE, d), lambda i, j: (i, 0)),
                  pl.BlockSpec((None, d, QKV_TN), lambda i, j: (layer, 0, j)),
                  pl.BlockSpec((ROW_TILE, HEAD_DIM), lambda i, j: (i, 0)),
                  pl.BlockSpec((ROW_TILE, HEAD_DIM), lambda i, j: (i, 0))],
        out_specs=pl.BlockSpec((ROW_TILE, QKV_TN), lambda i, j: (i, j)),
        out_shape=jax.ShapeDtypeStruct((m, n), F32),
        compiler_params=_params("parallel", "arbitrary"),
        name="qkv_rope",
    )(h, w, cosf, sinf)


def rope_tables(pos):
    inv = ROPE_THETA ** (-jnp.arange(0, HEAD_DIM, 2, dtype=F32) / HEAD_DIM)
    ang = pos.astype(F32)[:, None] * inv[None, :]
    cos = jnp.cos(ang)
    sin = jnp.sin(ang)
    return jnp.concatenate([cos, cos], axis=-1), jnp.concatenate([-sin, sin], axis=-1)


def _swa_prompt_kernel(sink_ref, q_ref, kp_ref, kc_ref, vp_ref, vc_ref, o_ref):
    g = pl.program_id(1)
    n = pl.program_id(2)
    q = jnp.concatenate([q_ref[:, j * HEAD_DIM:(j + 1) * HEAD_DIM] for j in range(GROUP)], axis=0).astype(BF16)
    k = jnp.concatenate([kp_ref[...], kc_ref[...]], axis=0).astype(BF16)
    v = jnp.concatenate([vp_ref[...], vc_ref[...]], axis=0).astype(BF16)
    s = lax.dot_general(k, q, NT_DIMS, preferred_element_type=F32) * SCALE
    key = lax.broadcasted_iota(jnp.int32, s.shape, 0)
    query = lax.broadcasted_iota(jnp.int32, s.shape, 1) % SWA_QT
    mask = (key > query) & (key <= query + WINDOW) & ((key >= WINDOW) | (n > 0))
    s = jnp.where(mask, s, NEG_INF)
    head = lax.broadcasted_iota(jnp.int32, (1, s.shape[1]), 1) // SWA_QT
    sink = jnp.zeros((1, s.shape[1]), F32)
    for j in range(GROUP):
        sink = jnp.where(head == j, sink_ref[g * GROUP + j], sink)
    m = jnp.maximum(jnp.max(s, axis=0, keepdims=True), sink)
    p = jnp.exp(s - m)
    denom = jnp.sum(p, axis=0, keepdims=True) + jnp.exp(sink - m)
    out = lax.dot_general(v, p.astype(BF16), TN_DIMS, preferred_element_type=F32) / denom
    for j in range(GROUP):
        o_ref[:, j * HEAD_DIM:(j + 1) * HEAD_DIM] = out[:, j * SWA_QT:(j + 1) * SWA_QT].T.astype(o_ref.dtype)


def swa_prompt(qkv, sinks, batch, seq):
    nt = seq // SWA_QT
    wpt = SWA_QT // WINDOW
    kcol = N_HEADS
    vcol = N_HEADS + N_KV_HEADS
    qw = GROUP * HEAD_DIM

    def prev_window(col):
        return pl.BlockSpec((WINDOW, HEAD_DIM),
                            lambda b, g, n: ((b * nt + n) * wpt - jnp.minimum(n, 1), col + g))

    def tile(col):
        return pl.BlockSpec((SWA_QT, HEAD_DIM), lambda b, g, n: (b * nt + n, col + g))

    return pl.pallas_call(
        _swa_prompt_kernel,
        grid=(batch, N_KV_HEADS, nt),
        in_specs=[
            pl.BlockSpec(memory_space=pltpu.SMEM),
            pl.BlockSpec((SWA_QT, qw), lambda b, g, n: (b * nt + n, g)),
            prev_window(kcol), tile(kcol), prev_window(vcol), tile(vcol),
        ],
        out_specs=pl.BlockSpec((SWA_QT, qw), lambda b, g, n: (b * nt + n, g)),
        out_shape=jax.ShapeDtypeStruct((batch * seq, N_HEADS * HEAD_DIM), BF16),
        compiler_params=_params("parallel", "parallel", "arbitrary"),
        name="swa_prompt",
    )(sinks, qkv, qkv, qkv, qkv, qkv)


def _moba_prompt_kernel(q_ref, k_ref, v_ref, o_ref, means_ref, sel_ref, acc_ref, *, nblk):
    c = pl.program_id(2)
    nq = GROUP * MOBA_BLOCK

    @pl.when(c == 0)
    def _():
        for n in range(nblk):
            kb = k_ref[n * MOBA_BLOCK:(n + 1) * MOBA_BLOCK, :]
            means_ref[n:n + 1, :] = jnp.sum(kb, axis=0, keepdims=True) / MOBA_BLOCK

    qf = jnp.concatenate([q_ref[:, j * HEAD_DIM:(j + 1) * HEAD_DIM] for j in range(GROUP)], axis=0)
    q = qf.astype(BF16)

    gate = lax.dot_general(means_ref[...], qf, NT_DIMS,
                           precision=lax.Precision.HIGHEST, preferred_element_type=F32)
    blk = lax.broadcasted_iota(jnp.int32, gate.shape, 0)
    gate = jnp.where(blk < c, gate, NEG_INF)
    for n in range(nblk):
        gate_n = gate[n:n + 1, :]
        ahead = (gate > gate_n) | ((gate == gate_n) & (blk < n))
        rank = jnp.sum(jnp.where(ahead, 1.0, 0.0), axis=0, keepdims=True)
        sel_ref[n:n + 1, :] = jnp.where(rank < MOBA_TOPK, 1.0, 0.0)

    def block_scores(off):
        kb = k_ref[pl.ds(off, MOBA_BLOCK), :].astype(BF16)
        return lax.dot_general(kb, q, NT_DIMS, preferred_element_type=F32) * SCALE

    def weighted_values(off, p):
        vb = v_ref[pl.ds(off, MOBA_BLOCK), :].astype(BF16)
        return lax.dot_general(vb, p.astype(BF16), TN_DIMS, preferred_element_type=F32)

    own = pl.multiple_of(c * MOBA_BLOCK, MOBA_BLOCK)
    s = block_scores(own)
    key = lax.broadcasted_iota(jnp.int32, s.shape, 0)
    query = lax.broadcasted_iota(jnp.int32, s.shape, 1) % MOBA_BLOCK
    s = jnp.where(key <= query, s, NEG_INF)
    m0 = jnp.max(s, axis=0, keepdims=True)
    p = jnp.exp(s - m0)
    l0 = jnp.sum(p, axis=0, keepdims=True)
    acc_ref[...] = weighted_values(own, p)

    def past_block(n, carry):
        m, l = carry
        off = pl.multiple_of(n * MOBA_BLOCK, MOBA_BLOCK)
        sb = jnp.where(sel_ref[pl.ds(n, 1), :] > 0.0, block_scores(off), NEG_INF)
        m_new = jnp.maximum(m, jnp.max(sb, axis=0, keepdims=True))
        alpha = jnp.exp(m - m_new)
        pb = jnp.exp(sb - m_new)
        acc_ref[...] = alpha * acc_ref[...] + weighted_values(off, pb)
        return m_new, alpha * l + jnp.sum(pb, axis=0, keepdims=True)

    _, l = lax.fori_loop(0, c, past_block, (m0, l0))
    out = acc_ref[...] / l
    for j in range(GROUP):
        o_ref[:, j * HEAD_DIM:(j + 1) * HEAD_DIM] = out[:, j * MOBA_BLOCK:(j + 1) * MOBA_BLOCK].T.astype(o_ref.dtype)


def moba_prompt(qkv, batch, seq):
    nblk = seq // MOBA_BLOCK
    kcol = N_HEADS
    vcol = N_HEADS + N_KV_HEADS
    qw = GROUP * HEAD_DIM
    nq = GROUP * MOBA_BLOCK
    return pl.pallas_call(
        functools.partial(_moba_prompt_kernel, nblk=nblk),
        grid=(batch, N_KV_HEADS, nblk),
        in_specs=[
            pl.BlockSpec((MOBA_BLOCK, qw), lambda b, g, c: (b * nblk + c, g)),
            pl.BlockSpec((seq, HEAD_DIM), lambda b, g, c: (b, kcol + g)),
            pl.BlockSpec((seq, HEAD_DIM), lambda b, g, c: (b, vcol + g)),
        ],
        out_specs=pl.BlockSpec((MOBA_BLOCK, qw), lambda b, g, c: (b * nblk + c, g)),
        out_shape=jax.ShapeDtypeStruct((batch * seq, N_HEADS * HEAD_DIM), BF16),
        scratch_shapes=[pltpu.VMEM((nblk, HEAD_DIM), F32),
                        pltpu.VMEM((nblk, nq), F32),
                        pltpu.VMEM((HEAD_DIM, nq), F32)],
        compiler_params=_params("parallel", "parallel", "arbitrary"),
        name="moba_prompt",
    )(qkv, qkv, qkv)


def _swa_sample_kernel(sink_ref, q_ref, kn_ref, vn_ref, ck_ref, cv_ref, o_ref, nk_ref, nv_ref):
    keep = WINDOW - DEC_SEQ
    nk_ref[0, 0:keep] = ck_ref[0, DEC_SEQ:WINDOW]
    nk_ref[0, keep:WINDOW] = kn_ref[0, 0:DEC_SEQ]
    nv_ref[0, 0:keep] = cv_ref[0, DEC_SEQ:WINDOW]
    nv_ref[0, keep:WINDOW] = vn_ref[0, 0:DEC_SEQ]

    t = lax.broadcasted_iota(jnp.int32, (ROWS_PER_KV, 2 * WINDOW), 0) % DEC_SEQ
    col = lax.broadcasted_iota(jnp.int32, (ROWS_PER_KV, 2 * WINDOW), 1)
    mask = ((col < WINDOW) & (col > t)) | ((col >= WINDOW) & (col - WINDOW <= t))
    row = lax.broadcasted_iota(jnp.int32, (ROWS_PER_KV, 1), 0)
    for g in range(N_KV_HEADS):
        q = q_ref[0, g].astype(BF16)
        k = jnp.concatenate([ck_ref[0, :, g, :], kn_ref[0, :, g, :]], axis=0).astype(BF16)
        v = jnp.concatenate([cv_ref[0, :, g, :], vn_ref[0, :, g, :]], axis=0).astype(BF16)
        s = lax.dot_general(q, k, NT_DIMS, preferred_element_type=F32) * SCALE
        s = jnp.where(mask, s, NEG_INF)
        sink = jnp.zeros((ROWS_PER_KV, 1), F32)
        for j in range(GROUP):
            sink = jnp.where(row // DEC_SEQ == j, sink_ref[g * GROUP + j], sink)
        m = jnp.maximum(jnp.max(s, axis=-1, keepdims=True), sink)
        p = jnp.exp(s - m)
        denom = jnp.sum(p, axis=-1, keepdims=True) + jnp.exp(sink - m)
        o_ref[0, g] = jnp.dot(p.astype(BF16), v, preferred_element_type=F32) / denom


def swa_sample(sinks, q_rows, k_own, v_own, cache_k, cache_v):
    nb = q_rows.shape[0]
    page_spec = pl.BlockSpec((1, PAGE_SIZE, N_KV_HEADS, HEAD_DIM), lambda b: (b, 0, 0, 0))
    q_spec = pl.BlockSpec((1, N_KV_HEADS, ROWS_PER_KV, HEAD_DIM), lambda b: (b, 0, 0, 0))
    cache_shape = jax.ShapeDtypeStruct((nb, WINDOW, N_KV_HEADS, HEAD_DIM), F32)
    return pl.pallas_call(
        _swa_sample_kernel,
        grid=(nb,),
        in_specs=[pl.BlockSpec(memory_space=pltpu.SMEM), q_spec, page_spec, page_spec, page_spec, page_spec],
        out_specs=[q_spec, page_spec, page_spec],
        out_shape=[jax.ShapeDtypeStruct(q_rows.shape, F32), cache_shape, cache_shape],
        compiler_params=_params("parallel"),
        name="swa_sample",
    )(sinks, q_rows, k_own, v_own, cache_k, cache_v)


def _head_concat(flat_ref, first_token, n_tokens):
    return jnp.concatenate(
        [flat_ref[pl.ds(first_token * N_KV_HEADS + g, n_tokens, stride=N_KV_HEADS), :]
         for g in range(N_KV_HEADS)], axis=1)


def _lane_to_rows(row_vec):
    i = lax.broadcasted_iota(jnp.int32, (LANES, LANES), 0)
    j = lax.broadcasted_iota(jnp.int32, (LANES, LANES), 1)
    return jnp.sum(jnp.where(i == j, row_vec, 0.0), axis=1, keepdims=True)


def _moba_sample_kernel(pt_ref, qbd_ref, kn_ref, vn_ref, kpool, vpool, o_ref,
                        buf, sem, scores, means, sel, acc, *, n_pages):
    b = pl.program_id(0)
    chunk_tokens = PAGES_PER_CHUNK * PAGE_SIZE
    n_chunks = n_pages // PAGES_PER_CHUNK
    blocks_per_chunk = chunk_tokens // MOBA_BLOCK
    n_past = n_pages * PAGE_SIZE // MOBA_BLOCK

    def chunk_copy(pool, ci, p):
        page = pt_ref[b * n_pages + ci * PAGES_PER_CHUNK + p]
        slot = ci % CHUNK_SLOTS
        return pltpu.make_async_copy(pool.at[page], buf.at[slot, pl.ds(p * PAGE_ROWS, PAGE_ROWS)],
                                     sem.at[slot, p])

    def start_chunk(pool, ci):
        for p in range(PAGES_PER_CHUNK):
            chunk_copy(pool, ci, p).start()

    def wait_chunk(pool, ci):
        for p in range(PAGES_PER_CHUNK):
            chunk_copy(pool, ci, p).wait()

    def prime(pool):
        for ci in range(CHUNK_SLOTS - 1):
            start_chunk(pool, ci)

    def stream(pool, per_block):
        def body(ci, carry):
            @pl.when(ci + CHUNK_SLOTS - 1 < n_chunks)
            def _():
                start_chunk(pool, ci + CHUNK_SLOTS - 1)

            wait_chunk(pool, ci)
            flat = buf.at[ci % CHUNK_SLOTS]
            for h in range(blocks_per_chunk):
                per_block(ci * blocks_per_chunk + h, _head_concat(flat, h * MOBA_BLOCK, MOBA_BLOCK))
            return carry

        lax.fori_loop(0, n_chunks, body, 0)

    def scores_of(rows, qbd):
        return jnp.dot(rows.astype(BF16), qbd, preferred_element_type=F32) * SCALE

    def weighted_values(p, rows):
        return lax.dot_general(p.astype(BF16), rows.astype(BF16), TN_DIMS, preferred_element_type=F32)

    qbd_f32 = qbd_ref[0]
    qbd = qbd_f32.astype(BF16)

    def k_block(n, rows):
        means[pl.ds(n, 1), :] = jnp.sum(rows, axis=0, keepdims=True) / MOBA_BLOCK
        off = pl.multiple_of(n * MOBA_BLOCK, MOBA_BLOCK)
        scores[pl.ds(off, MOBA_BLOCK), :] = scores_of(rows, qbd)

    prime(kpool)
    stream(kpool, k_block)
    prime(vpool)

    gate = jnp.dot(means[...], qbd_f32, precision=lax.Precision.HIGHEST, preferred_element_type=F32)
    blk = lax.broadcasted_iota(jnp.int32, gate.shape, 0).astype(F32)
    chosen = jnp.zeros(gate.shape, F32)
    for _ in range(MOBA_TOPK):
        best = jnp.max(gate, axis=0, keepdims=True)
        first = jnp.min(jnp.where(gate == best, blk, float(n_past)), axis=0, keepdims=True)
        hit = blk == first
        chosen = jnp.where(hit, 1.0, chosen)
        gate = jnp.where(hit, NEG_INF, gate)
    sel[...] = chosen

    own_k = _head_concat(kn_ref.at[0], 0, PAGE_SIZE)
    own_v = _head_concat(vn_ref.at[0], 0, PAGE_SIZE)
    s_own = scores_of(own_k, qbd)
    row = lax.broadcasted_iota(jnp.int32, s_own.shape, 0)
    t = lax.broadcasted_iota(jnp.int32, s_own.shape, 1) % DEC_SEQ
    s_own = jnp.where(row <= t, s_own, NEG_INF)

    def block_scores(n):
        off = pl.multiple_of(n * MOBA_BLOCK, MOBA_BLOCK)
        return off, jnp.where(sel[pl.ds(n, 1), :] > 0.0, scores[pl.ds(off, MOBA_BLOCK), :], NEG_INF)

    def max_body(n, m):
        return jnp.maximum(m, jnp.max(block_scores(n)[1], axis=0, keepdims=True))

    m = lax.fori_loop(0, n_past, max_body, jnp.max(s_own, axis=0, keepdims=True))
    p_own = jnp.exp(s_own - m)

    def exp_body(n, l):
        off, s = block_scores(n)
        p = jnp.exp(s - m)
        scores[pl.ds(off, MOBA_BLOCK), :] = p
        return l + jnp.sum(p, axis=0, keepdims=True)

    l = lax.fori_loop(0, n_past, exp_body, jnp.sum(p_own, axis=0, keepdims=True))

    acc[...] = weighted_values(p_own, own_v)

    def v_block(n, rows):
        off = pl.multiple_of(n * MOBA_BLOCK, MOBA_BLOCK)
        acc[...] += weighted_values(scores[pl.ds(off, MOBA_BLOCK), :], rows)

    stream(vpool, v_block)

    inv = 1.0 / _lane_to_rows(l)
    for g in range(N_KV_HEADS):
        rows = slice(g * ROWS_PER_KV, (g + 1) * ROWS_PER_KV)
        o_ref[0, g] = acc[rows, g * HEAD_DIM:(g + 1) * HEAD_DIM] * inv[rows]


def moba_sample(page_table, qbd, k_own, v_own, pool_k, pool_v):
    nb, n_pages = page_table.shape
    n_past = n_pages * PAGE_SIZE // MOBA_BLOCK
    page_spec = pl.BlockSpec((1, PAGE_ROWS, HEAD_DIM), lambda b, pt: (b, 0, 0))
    grid_spec = pltpu.PrefetchScalarGridSpec(
        num_scalar_prefetch=1,
        grid=(nb,),
        in_specs=[pl.BlockSpec((1, KV_DIM, LANES), lambda b, pt: (b, 0, 0)), page_spec, page_spec,
                  pl.BlockSpec(memory_space=pl.ANY), pl.BlockSpec(memory_space=pl.ANY)],
        out_specs=pl.BlockSpec((1, N_KV_HEADS, ROWS_PER_KV, HEAD_DIM), lambda b, pt: (b, 0, 0, 0)),
        scratch_shapes=[
            pltpu.VMEM((CHUNK_SLOTS, PAGES_PER_CHUNK * PAGE_ROWS, HEAD_DIM), F32),
            pltpu.SemaphoreType.DMA((CHUNK_SLOTS, PAGES_PER_CHUNK)),
            pltpu.VMEM((n_past * MOBA_BLOCK, LANES), F32),
            pltpu.VMEM((n_past, KV_DIM), F32),
            pltpu.VMEM((n_past, LANES), F32),
            pltpu.VMEM((LANES, KV_DIM), F32),
        ],
    )
    return pl.pallas_call(
        functools.partial(_moba_sample_kernel, n_pages=n_pages),
        grid_spec=grid_spec,
        out_shape=jax.ShapeDtypeStruct((nb, N_KV_HEADS, ROWS_PER_KV, HEAD_DIM), F32),
        compiler_params=_params("arbitrary"),
        name="moba_sample",
    )(page_table.reshape(-1), qbd, k_own, v_own, pool_k, pool_v)


def sample_query_rows(q):
    nb, t, _ = q.shape
    q = q.reshape(nb, t, N_KV_HEADS, GROUP, HEAD_DIM).transpose(0, 2, 3, 1, 4)
    return q.reshape(nb, N_KV_HEADS, GROUP * t, HEAD_DIM)


def block_diag_queries(q_rows):
    nb = q_rows.shape[0]
    eye = jnp.eye(N_KV_HEADS, dtype=q_rows.dtype)
    return jnp.einsum("bgrd,gh->bgdhr", q_rows, eye).reshape(nb, KV_DIM, N_KV_HEADS * ROWS_PER_KV)


def heads_from_rows(o, t):
    nb = o.shape[0]
    o = o.reshape(nb, N_KV_HEADS, GROUP, t, HEAD_DIM).transpose(0, 3, 1, 2, 4)
    return o.reshape(nb * t, N_HEADS * HEAD_DIM)


def own_page(x):
    nb, t, _ = x.shape
    x = x.reshape(nb, t, N_KV_HEADS, HEAD_DIM)
    return jnp.pad(x, ((0, 0), (0, PAGE_SIZE - t), (0, 0), (0, 0)))


def kernel(x_prompt, x_sample, cache_swa_k, cache_swa_v, cache_moba_k, cache_moba_v, page_table, attn_norm, w_qkv, w_o, swa_sinks, ffn1_norm, ffn1_w_gate, ffn1_w_up, ffn1_w_down, ffn2_norm, ffn2_w_gate, ffn2_w_up, ffn2_w_down, final_norm):
    batch, seq, d = x_prompt.shape
    dec_batch, dec_seq, _ = x_sample.shape
    assert dec_seq == DEC_SEQ
    mp = batch * seq
    depth = w_qkv.shape[0]
    q_dim = N_HEADS * HEAD_DIM

    x = jnp.concatenate([x_prompt.reshape(mp, d), x_sample.reshape(dec_batch * dec_seq, d)], axis=0)
    pos = jnp.concatenate([jnp.tile(jnp.arange(seq), batch), jnp.tile(PAST_LEN + jnp.arange(dec_seq), dec_batch)])
    cosf, sinf = rope_tables(pos)

    ffn = []
    for norm, wg, wu, wd in ((ffn1_norm, ffn1_w_gate, ffn1_w_up, ffn1_w_down),
                             (ffn2_norm, ffn2_w_gate, ffn2_w_up, ffn2_w_down)):
        ffn.append((norm, wg, wu, cast_weights(wd, D_FF_PAD, d)))
    w_qkv_b = cast_weights(w_qkv, d, w_qkv.shape[2])
    w_o_b = cast_weights(w_o, q_dim, d)

    def half_ffn(x, which, layer):
        norm, wg, wu, wd = ffn[which]
        a = gate_up(rmsnorm(x, norm[layer], BF16), wg, wu, layer)
        return matmul_residual(a, wd, layer, x, DOWN_TN, DOWN_TK)

    outs = {}
    for i in range(depth):
        a_idx = i // 2
        x = half_ffn(x, 0, i)
        qkv = qkv_rope(rmsnorm(x, attn_norm[i], BF16), w_qkv_b, i, cosf, sinf)
        qkv_s = qkv[mp:].reshape(dec_batch, dec_seq, -1)
        k_new_p = qkv[:mp, q_dim:q_dim + KV_DIM].reshape(batch, seq, N_KV_HEADS, HEAD_DIM)
        v_new_p = qkv[:mp, q_dim + KV_DIM:].reshape(batch, seq, N_KV_HEADS, HEAD_DIM)
        k_new_s = qkv_s[..., q_dim:q_dim + KV_DIM]
        v_new_s = qkv_s[..., q_dim + KV_DIM:]
        q_rows = sample_query_rows(qkv_s[..., :q_dim])
        if i % 2 == 0:
            o_p = swa_prompt(qkv, swa_sinks[a_idx], batch, seq)
            o_s, nk, nv = swa_sample(swa_sinks[a_idx], q_rows, own_page(k_new_s), own_page(v_new_s),
                                     cache_swa_k[a_idx], cache_swa_v[a_idx])
            outs.setdefault("swa_kp", []).append(k_new_p[:, -WINDOW:])
            outs.setdefault("swa_vp", []).append(v_new_p[:, -WINDOW:])
            outs.setdefault("swa_ks", []).append(nk)
            outs.setdefault("swa_vs", []).append(nv)
        else:
            o_p = moba_prompt(qkv, batch, seq)
            flat = (-1, PAGE_ROWS, HEAD_DIM)
            o_s = moba_sample(page_table, block_diag_queries(q_rows),
                              own_page(k_new_s).reshape(flat), own_page(v_new_s).reshape(flat),
                              cache_moba_k[a_idx].reshape(flat), cache_moba_v[a_idx].reshape(flat))
            outs.setdefault("moba_kp", []).append(k_new_p)
            outs.setdefault("moba_vp", []).append(v_new_p)
            outs.setdefault("moba_ks", []).append(k_new_s.reshape(dec_batch, dec_seq, N_KV_HEADS, HEAD_DIM))
            outs.setdefault("moba_vs", []).append(v_new_s.reshape(dec_batch, dec_seq, N_KV_HEADS, HEAD_DIM))
        o = jnp.concatenate([o_p, heads_from_rows(o_s, dec_seq).astype(BF16)], axis=0)
        x = matmul_residual(o, w_o_b, i, x, OUT_TN, OUT_TK)
        x = half_ffn(x, 1, i)

    ms = dec_batch * dec_seq
    y_prompt = rmsnorm(x, final_norm, F32, tile=FINAL_TILE, rows=mp).reshape(batch, seq, d)
    y_sample = rmsnorm(x, final_norm, F32, tile=ms, first_tile=mp // ms, rows=ms).reshape(dec_batch, dec_seq, d)
    return (y_prompt, y_sample,
            jnp.stack(outs["swa_kp"]), jnp.stack(outs["swa_vp"]),
            jnp.stack(outs["swa_ks"]), jnp.stack(outs["swa_vs"]),
            jnp.stack(outs["moba_kp"]), jnp.stack(outs["moba_vp"]),
            jnp.stack(outs["moba_ks"]), jnp.stack(outs["moba_vs"]))
```

```python
import functools

import jax
import jax.numpy as jnp
from jax import lax
from jax.experimental import pallas as pl
from jax.experimental.pallas import tpu as pltpu

F32 = jnp.float32
BF16 = jnp.bfloat16

D_MODEL = 4096
N_HEADS = 32
N_KV_HEADS = 8
HEAD_DIM = 128
GROUP = N_HEADS // N_KV_HEADS
KV_DIM = N_KV_HEADS * HEAD_DIM
D_FF = 11008
WINDOW = 128
MOBA_BLOCK = 256
MOBA_TOPK = 3
PAGE_SIZE = 128
PAGE_ROWS = PAGE_SIZE * N_KV_HEADS
PAST_LEN = 16384
DEC_SEQ = 4
ROPE_THETA = 10000.0
RMS_EPS = 1e-6
SCALE = HEAD_DIM ** -0.5

V7X_VMEM_BYTES = 64 * 1024 * 1024
VMEM_LIMIT = V7X_VMEM_BYTES - 8 * 1024 * 1024
LANES = 128

ROW_TILE = 1040
NORM_TILE = 416
FINAL_TILE = 512
SWA_QT = 256
GATE_ROW_TILE = 2080
FF_TILE = 256
D_FF_PAD = 11264
DOWN_TN = 1024
DOWN_TK = 2816
QKV_TN = 512
OUT_TN = 1024
OUT_TK = 2048
PAGES_PER_CHUNK = 4
CHUNK_SLOTS = 4
ROWS_PER_KV = GROUP * DEC_SEQ
NEG_INF = float("-inf")
NT_DIMS = (((1,), (1,)), ((), ()))
TN_DIMS = (((0,), (0,)), ((), ()))


def _params(*sem):
    return pltpu.CompilerParams(dimension_semantics=sem, vmem_limit_bytes=VMEM_LIMIT)


def _rms_kernel(x_ref, g_ref, o_ref):
    x = x_ref[...]
    ms = jnp.mean(x * x, axis=-1, keepdims=True)
    o_ref[...] = ((x * lax.rsqrt(ms + RMS_EPS)) * g_ref[...]).astype(o_ref.dtype)


def rmsnorm(x, gain, out_dtype, tile=NORM_TILE, first_tile=0, rows=None):
    d = x.shape[1]
    rows = x.shape[0] if rows is None else rows
    return pl.pallas_call(
        _rms_kernel,
        grid=(rows // tile,),
        in_specs=[pl.BlockSpec((tile, d), lambda i: (first_tile + i, 0)),
                  pl.BlockSpec((1, d), lambda i: (0, 0))],
        out_specs=pl.BlockSpec((tile, d), lambda i: (i, 0)),
        out_shape=jax.ShapeDtypeStruct((rows, d), out_dtype),
        compiler_params=_params("parallel"),
        name="rmsnorm",
    )(x, gain.reshape(1, d))


def _gateup_kernel(h_ref, wg_ref, wu_ref, wd_ref, o_ref, wdb_ref, *, real_tiles):
    i = pl.program_id(0)
    j = pl.program_id(1)

    @pl.when(j < real_tiles)
    def _():
        h = h_ref[...]
        g = jnp.dot(h, wg_ref[...].astype(BF16), preferred_element_type=F32)
        u = jnp.dot(h, wu_ref[...].astype(BF16), preferred_element_type=F32)
        o_ref[...] = (0.5 * g * jax.nn.sigmoid(g) * u).astype(o_ref.dtype)

    @pl.when(j >= real_tiles)
    def _():
        o_ref[...] = jnp.zeros_like(o_ref)

    @pl.when((i == 0) & (j < real_tiles))
    def _():
        wdb_ref[...] = wd_ref[...].astype(BF16)

    @pl.when((i == 0) & (j >= real_tiles))
    def _():
        wdb_ref[...] = jnp.zeros_like(wdb_ref)


def gate_up(h, wg, wu, wd, layer):
    m, d = h.shape
    real_tiles = wg.shape[2] // FF_TILE
    pad_tiles = D_FF_PAD // FF_TILE
    assert pad_tiles == real_tiles + 1

    def real(j):
        return jnp.minimum(j, real_tiles - 1)

    w_spec = pl.BlockSpec((None, d, FF_TILE), lambda i, j: (layer, 0, real(j)))
    wd_spec = pl.BlockSpec((None, FF_TILE, d), lambda i, j: (layer, jnp.where(i == 0, real(j), real_tiles - 1), 0))
    wdb_spec = pl.BlockSpec((FF_TILE, d), lambda i, j: (jnp.where(i == 0, j, real_tiles), 0))
    return pl.pallas_call(
        functools.partial(_gateup_kernel, real_tiles=real_tiles),
        grid=(m // GATE_ROW_TILE, pad_tiles),
        in_specs=[pl.BlockSpec((GATE_ROW_TILE, d), lambda i, j: (i, 0), pipeline_mode=pl.Buffered(1)),
                  w_spec, w_spec, wd_spec],
        out_specs=[pl.BlockSpec((GATE_ROW_TILE, FF_TILE), lambda i, j: (i, j)), wdb_spec],
        out_shape=[jax.ShapeDtypeStruct((m, D_FF_PAD), BF16), jax.ShapeDtypeStruct((D_FF_PAD, d), BF16)],
        compiler_params=_params("arbitrary", "arbitrary"),
        name="gate_up",
    )(h, wg, wu, wd)


def _matmul_residual_kernel(a_ref, w_ref, r_ref, o_ref):
    k = pl.program_id(2)

    @pl.when(k == 0)
    def _():
        o_ref[...] = r_ref[...] + jnp.dot(a_ref[...], w_ref[...].astype(BF16), preferred_element_type=F32)

    @pl.when(k > 0)
    def _():
        o_ref[...] += jnp.dot(a_ref[...], w_ref[...].astype(BF16), preferred_element_type=F32)


def matmul_residual(a, w, layer, res, tn, tk):
    m, kdim = a.shape
    n = w.shape[2]
    return pl.pallas_call(
        _matmul_residual_kernel,
        grid=(m // ROW_TILE, n // tn, kdim // tk),
        in_specs=[pl.BlockSpec((ROW_TILE, tk), lambda i, j, k: (i, k)),
                  pl.BlockSpec((None, tk, tn), lambda i, j, k: (layer, k, j)),
                  pl.BlockSpec((ROW_TILE, tn), lambda i, j, k: (i, j))],
        out_specs=pl.BlockSpec((ROW_TILE, tn), lambda i, j, k: (i, j)),
        out_shape=jax.ShapeDtypeStruct((m, n), F32),
        compiler_params=_params("parallel", "parallel", "arbitrary"),
        name="matmul_residual",
    )(a, w, res)


def _qkv_kernel(h_ref, w_ref, cos_ref, sin_ref, o_ref, *, rope_tiles):
    acc = jnp.dot(h_ref[...], w_ref[...].astype(BF16), preferred_element_type=F32)
    n = pl.program_id(1)

    @pl.when(n < rope_tiles)
    def _():
        cosf = cos_ref[...]
        sinf = sin_ref[...]
        for c in range(acc.shape[1] // HEAD_DIM):
            x = acc[:, c * HEAD_DIM:(c + 1) * HEAD_DIM]
            o_ref[:, c * HEAD_DIM:(c + 1) * HEAD_DIM] = x * cosf + pltpu.roll(x, HEAD_DIM // 2, 1) * sinf

    @pl.when(n >= rope_tiles)
    def _():
        o_ref[...] = acc


def qkv_rope(h, w, layer, cosf, sinf):
    m, d = h.shape
    n = w.shape[2]
    rope_tiles = (N_HEADS + N_KV_HEADS) * HEAD_DIM // QKV_TN
    return pl.pallas_call(
        functools.partial(_qkv_kernel, rope_tiles=rope_tiles),
        grid=(m // ROW_TILE, n // QKV_TN),
        in_specs=[pl.BlockSpec((ROW_TILE, d), lambda i, j: (i, 0)),
                  pl.BlockSpec((None, d, QKV_TN), lambda i, j: (layer, 0, j)),
                  pl.BlockSpec((ROW_TILE, HEAD_DIM), lambda i, j: (i, 0)),
                  pl.BlockSpec((ROW_TILE, HEAD_DIM), lambda i, j: (i, 0))],
        out_specs=pl.BlockSpec((ROW_TILE, QKV_TN), lambda i, j: (i, j)),
        out_shape=jax.ShapeDtypeStruct((m, n), F32),
        compiler_params=_params("parallel", "arbitrary"),
        name="qkv_rope",
    )(h, w, cosf, sinf)


def rope_tables(pos):
    inv = ROPE_THETA ** (-jnp.arange(0, HEAD_DIM, 2, dtype=F32) / HEAD_DIM)
    ang = pos.astype(F32)[:, None] * inv[None, :]
    cos = jnp.cos(ang)
    sin = jnp.sin(ang)
    return jnp.concatenate([cos, cos], axis=-1), jnp.concatenate([-sin, sin], axis=-1)


def _swa_prompt_kernel(sink_ref, q_ref, kp_ref, kc_ref, vp_ref, vc_ref, o_ref):
    g = pl.program_id(1)
    n = pl.program_id(2)
    q = jnp.concatenate([q_ref[:, j * HEAD_DIM:(j + 1) * HEAD_DIM] for j in range(GROUP)], axis=0).astype(BF16)
    k = jnp.concatenate([kp_ref[...], kc_ref[...]], axis=0).astype(BF16)
    v = jnp.concatenate([vp_ref[...], vc_ref[...]], axis=0).astype(BF16)
    s = lax.dot_general(k, q, NT_DIMS, preferred_element_type=F32) * SCALE
    key = lax.broadcasted_iota(jnp.int32, s.shape, 0)
    query = lax.broadcasted_iota(jnp.int32, s.shape, 1) % SWA_QT
    mask = (key > query) & (key <= query + WINDOW) & ((key >= WINDOW) | (n > 0))
    s = jnp.where(mask, s, NEG_INF)
    head = lax.broadcasted_iota(jnp.int32, (1, s.shape[1]), 1) // SWA_QT
    sink = jnp.zeros((1, s.shape[1]), F32)
    for j in range(GROUP):
        sink = jnp.where(head == j, sink_ref[g * GROUP + j], sink)
    m = jnp.maximum(jnp.max(s, axis=0, keepdims=True), sink)
    p = jnp.exp(s - m)
    denom = jnp.sum(p, axis=0, keepdims=True) + jnp.exp(sink - m)
    out = lax.dot_general(v, p.astype(BF16), TN_DIMS, preferred_element_type=F32) / denom
    for j in range(GROUP):
        o_ref[:, j * HEAD_DIM:(j + 1) * HEAD_DIM] = out[:, j * SWA_QT:(j + 1) * SWA_QT].T.astype(o_ref.dtype)


def swa_prompt(qkv, sinks, batch, seq):
    nt = seq // SWA_QT
    wpt = SWA_QT // WINDOW
    kcol = N_HEADS
    vcol = N_HEADS + N_KV_HEADS
    qw = GROUP * HEAD_DIM

    def prev_window(col):
        return pl.BlockSpec((WINDOW, HEAD_DIM),
                            lambda b, g, n: ((b * nt + n) * wpt - jnp.minimum(n, 1), col + g))

    def tile(col):
        return pl.BlockSpec((SWA_QT, HEAD_DIM), lambda b, g, n: (b * nt + n, col + g))

    return pl.pallas_call(
        _swa_prompt_kernel,
        grid=(batch, N_KV_HEADS, nt),
        in_specs=[
            pl.BlockSpec(memory_space=pltpu.SMEM),
            pl.BlockSpec((SWA_QT, qw), lambda b, g, n: (b * nt + n, g)),
            prev_window(kcol), tile(kcol), prev_window(vcol), tile(vcol),
        ],
        out_specs=pl.BlockSpec((SWA_QT, qw), lambda b, g, n: (b * nt + n, g)),
        out_shape=jax.ShapeDtypeStruct((batch * seq, N_HEADS * HEAD_DIM), BF16),
        compiler_params=_params("parallel", "parallel", "arbitrary"),
        name="swa_prompt",
    )(sinks, qkv, qkv, qkv, qkv, qkv)


def _moba_prompt_kernel(q_ref, k_ref, v_ref, o_ref, means_ref, sel_ref, acc_ref, *, nblk):
    c = pl.program_id(2)
    nq = GROUP * MOBA_BLOCK

    @pl.when(c == 0)
    def _():
        for n in range(nblk):
            kb = k_ref[n * MOBA_BLOCK:(n + 1) * MOBA_BLOCK, :]
            means_ref[n:n + 1, :] = jnp.sum(kb, axis=0, keepdims=True) / MOBA_BLOCK

    qf = jnp.concatenate([q_ref[:, j * HEAD_DIM:(j + 1) * HEAD_DIM] for j in range(GROUP)], axis=0)
    q = qf.astype(BF16)

    gate = lax.dot_general(means_ref[...], qf, NT_DIMS,
                           precision=lax.Precision.HIGHEST, preferred_element_type=F32)
    blk = lax.broadcasted_iota(jnp.int32, gate.shape, 0)
    gate = jnp.where(blk < c, gate, NEG_INF)
    for n in range(nblk):
        gate_n = gate[n:n + 1, :]
        ahead = (gate > gate_n) | ((gate == gate_n) & (blk < n))
        rank = jnp.sum(jnp.where(ahead, 1.0, 0.0), axis=0, keepdims=True)
        sel_ref[n:n + 1, :] = jnp.where(rank < MOBA_TOPK, 1.0, 0.0)

    def block_scores(off):
        kb = k_ref[pl.ds(off, MOBA_BLOCK), :].astype(BF16)
        return lax.dot_general(kb, q, NT_DIMS, preferred_element_type=F32) * SCALE

    def weighted_values(off, p):
        vb = v_ref[pl.ds(off, MOBA_BLOCK), :].astype(BF16)
        return lax.dot_general(vb, p.astype(BF16), TN_DIMS, preferred_element_type=F32)

    own = pl.multiple_of(c * MOBA_BLOCK, MOBA_BLOCK)
    s = block_scores(own)
    key = lax.broadcasted_iota(jnp.int32, s.shape, 0)
    query = lax.broadcasted_iota(jnp.int32, s.shape, 1) % MOBA_BLOCK
    s = jnp.where(key <= query, s, NEG_INF)
    m0 = jnp.max(s, axis=0, keepdims=True)
    p = jnp.exp(s - m0)
    l0 = jnp.sum(p, axis=0, keepdims=True)
    acc_ref[...] = weighted_values(own, p)

    def past_block(n, carry):
        m, l = carry
        off = pl.multiple_of(n * MOBA_BLOCK, MOBA_BLOCK)
        sb = jnp.where(sel_ref[pl.ds(n, 1), :] > 0.0, block_scores(off), NEG_INF)
        m_new = jnp.maximum(m, jnp.max(sb, axis=0, keepdims=True))
        alpha = jnp.exp(m - m_new)
        pb = jnp.exp(sb - m_new)
        acc_ref[...] = alpha * acc_ref[...] + weighted_values(off, pb)
        return m_new, alpha * l + jnp.sum(pb, axis=0, keepdims=True)

    _, l = lax.fori_loop(0, c, past_block, (m0, l0))
    out = acc_ref[...] / l
    for j in range(GROUP):
        o_ref[:, j * HEAD_DIM:(j + 1) * HEAD_DIM] = out[:, j * MOBA_BLOCK:(j + 1) * MOBA_BLOCK].T.astype(o_ref.dtype)


def moba_prompt(qkv, batch, seq):
    nblk = seq // MOBA_BLOCK
    kcol = N_HEADS
    vcol = N_HEADS + N_KV_HEADS
    qw = GROUP * HEAD_DIM
    nq = GROUP * MOBA_BLOCK
    return pl.pallas_call(
        functools.partial(_moba_prompt_kernel, nblk=nblk),
        grid=(batch, N_KV_HEADS, nblk),
        in_specs=[
            pl.BlockSpec((MOBA_BLOCK, qw), lambda b, g, c: (b * nblk + c, g)),
            pl.BlockSpec((seq, HEAD_DIM), lambda b, g, c: (b, kcol + g)),
            pl.BlockSpec((seq, HEAD_DIM), lambda b, g, c: (b, vcol + g)),
        ],
        out_specs=pl.BlockSpec((MOBA_BLOCK, qw), lambda b, g, c: (b * nblk + c, g)),
        out_shape=jax.ShapeDtypeStruct((batch * seq, N_HEADS * HEAD_DIM), BF16),
        scratch_shapes=[pltpu.VMEM((nblk, HEAD_DIM), F32),
                        pltpu.VMEM((nblk, nq), F32),
                        pltpu.VMEM((HEAD_DIM, nq), F32)],
        compiler_params=_params("parallel", "parallel", "arbitrary"),
        name="moba_prompt",
    )(qkv, qkv, qkv)


def _swa_sample_kernel(sink_ref, q_ref, kn_ref, vn_ref, ck_ref, cv_ref, o_ref, nk_ref, nv_ref):
    keep = WINDOW - DEC_SEQ
    nk_ref[0, 0:keep] = ck_ref[0, DEC_SEQ:WINDOW]
    nk_ref[0, keep:WINDOW] = kn_ref[0, 0:DEC_SEQ]
    nv_ref[0, 0:keep] = cv_ref[0, DEC_SEQ:WINDOW]
    nv_ref[0, keep:WINDOW] = vn_ref[0, 0:DEC_SEQ]

    t = lax.broadcasted_iota(jnp.int32, (ROWS_PER_KV, 2 * WINDOW), 0) % DEC_SEQ
    col = lax.broadcasted_iota(jnp.int32, (ROWS_PER_KV, 2 * WINDOW), 1)
    mask = ((col < WINDOW) & (col > t)) | ((col >= WINDOW) & (col - WINDOW <= t))
    row = lax.broadcasted_iota(jnp.int32, (ROWS_PER_KV, 1), 0)
    for g in range(N_KV_HEADS):
        q = q_ref[0, g].astype(BF16)
        k = jnp.concatenate([ck_ref[0, :, g, :], kn_ref[0, :, g, :]], axis=0).astype(BF16)
        v = jnp.concatenate([cv_ref[0, :, g, :], vn_ref[0, :, g, :]], axis=0).astype(BF16)
        s = lax.dot_general(q, k, NT_DIMS, preferred_element_type=F32) * SCALE
        s = jnp.where(mask, s, NEG_INF)
        sink = jnp.zeros((ROWS_PER_KV, 1), F32)
        for j in range(GROUP):
            sink = jnp.where(row // DEC_SEQ == j, sink_ref[g * GROUP + j], sink)
        m = jnp.maximum(jnp.max(s, axis=-1, keepdims=True), sink)
        p = jnp.exp(s - m)
        denom = jnp.sum(p, axis=-1, keepdims=True) + jnp.exp(sink - m)
        o_ref[0, g] = jnp.dot(p.astype(BF16), v, preferred_element_type=F32) / denom


def swa_sample(sinks, q_rows, k_own, v_own, cache_k, cache_v):
    nb = q_rows.shape[0]
    page_spec = pl.BlockSpec((1, PAGE_SIZE, N_KV_HEADS, HEAD_DIM), lambda b: (b, 0, 0, 0))
    q_spec = pl.BlockSpec((1, N_KV_HEADS, ROWS_PER_KV, HEAD_DIM), lambda b: (b, 0, 0, 0))
    cache_shape = jax.ShapeDtypeStruct((nb, WINDOW, N_KV_HEADS, HEAD_DIM), F32)
    return pl.pallas_call(
        _swa_sample_kernel,
        grid=(nb,),
        in_specs=[pl.BlockSpec(memory_space=pltpu.SMEM), q_spec, page_spec, page_spec, page_spec, page_spec],
        out_specs=[q_spec, page_spec, page_spec],
        out_shape=[jax.ShapeDtypeStruct(q_rows.shape, F32), cache_shape, cache_shape],
        compiler_params=_params("parallel"),
        name="swa_sample",
    )(sinks, q_rows, k_own, v_own, cache_k, cache_v)


def _head_concat(flat_ref, first_token, n_tokens):
    return jnp.concatenate(
        [flat_ref[pl.ds(first_token * N_KV_HEADS + g, n_tokens, stride=N_KV_HEADS), :]
         for g in range(N_KV_HEADS)], axis=1)


def _lane_to_rows(row_vec):
    i = lax.broadcasted_iota(jnp.int32, (LANES, LANES), 0)
    j = lax.broadcasted_iota(jnp.int32, (LANES, LANES), 1)
    return jnp.sum(jnp.where(i == j, row_vec, 0.0), axis=1, keepdims=True)


def _moba_sample_kernel(pt_ref, qbd_ref, kn_ref, vn_ref, kpool, vpool, o_ref,
                        buf, sem, scores, means, sel, acc, *, n_pages):
    b = pl.program_id(0)
    chunk_tokens = PAGES_PER_CHUNK * PAGE_SIZE
    n_chunks = n_pages // PAGES_PER_CHUNK
    blocks_per_chunk = chunk_tokens // MOBA_BLOCK
    n_past = n_pages * PAGE_SIZE // MOBA_BLOCK

    def chunk_copy(pool, ci, p):
        page = pt_ref[b * n_pages + ci * PAGES_PER_CHUNK + p]
        slot = ci % CHUNK_SLOTS
        return pltpu.make_async_copy(pool.at[page], buf.at[slot, pl.ds(p * PAGE_ROWS, PAGE_ROWS)],
                                     sem.at[slot, p])

    def start_chunk(pool, ci):
        for p in range(PAGES_PER_CHUNK):
            chunk_copy(pool, ci, p).start()

    def wait_chunk(pool, ci):
        for p in range(PAGES_PER_CHUNK):
            chunk_copy(pool, ci, p).wait()

    def prime(pool):
        for ci in range(CHUNK_SLOTS - 1):
            start_chunk(pool, ci)

    def stream(pool, per_block):
        def body(ci, carry):
            @pl.when(ci + CHUNK_SLOTS - 1 < n_chunks)
            def _():
                start_chunk(pool, ci + CHUNK_SLOTS - 1)

            wait_chunk(pool, ci)
            flat = buf.at[ci % CHUNK_SLOTS]
            for h in range(blocks_per_chunk):
                per_block(ci * blocks_per_chunk + h, _head_concat(flat, h * MOBA_BLOCK, MOBA_BLOCK))
            return carry

        lax.fori_loop(0, n_chunks, body, 0)

    def scores_of(rows, qbd):
        return jnp.dot(rows.astype(BF16), qbd, preferred_element_type=F32) * SCALE

    def weighted_values(p, rows):
        return lax.dot_general(p.astype(BF16), rows.astype(BF16), TN_DIMS, preferred_element_type=F32)

    qbd_f32 = qbd_ref[0]
    qbd = qbd_f32.astype(BF16)

    def k_block(n, rows):
        means[pl.ds(n, 1), :] = jnp.sum(rows, axis=0, keepdims=True) / MOBA_BLOCK
        off = pl.multiple_of(n * MOBA_BLOCK, MOBA_BLOCK)
        scores[pl.ds(off, MOBA_BLOCK), :] = scores_of(rows, qbd)

    prime(kpool)
    stream(kpool, k_block)
    prime(vpool)

    gate = jnp.dot(means[...], qbd_f32, precision=lax.Precision.HIGHEST, preferred_element_type=F32)
    blk = lax.broadcasted_iota(jnp.int32, gate.shape, 0).astype(F32)
    chosen = jnp.zeros(gate.shape, F32)
    for _ in range(MOBA_TOPK):
        best = jnp.max(gate, axis=0, keepdims=True)
        first = jnp.min(jnp.where(gate == best, blk, float(n_past)), axis=0, keepdims=True)
        hit = blk == first
        chosen = jnp.where(hit, 1.0, chosen)
        gate = jnp.where(hit, NEG_INF, gate)
    sel[...] = chosen

    own_k = _head_concat(kn_ref.at[0], 0, PAGE_SIZE)
    own_v = _head_concat(vn_ref.at[0], 0, PAGE_SIZE)
    s_own = scores_of(own_k, qbd)
    row = lax.broadcasted_iota(jnp.int32, s_own.shape, 0)
    t = lax.broadcasted_iota(jnp.int32, s_own.shape, 1) % DEC_SEQ
    s_own = jnp.where(row <= t, s_own, NEG_INF)

    def block_scores(n):
        off = pl.multiple_of(n * MOBA_BLOCK, MOBA_BLOCK)
        return off, jnp.where(sel[pl.ds(n, 1), :] > 0.0, scores[pl.ds(off, MOBA_BLOCK), :], NEG_INF)

    def max_body(n, m):
        return jnp.maximum(m, jnp.max(block_scores(n)[1], axis=0, keepdims=True))

    m = lax.fori_loop(0, n_past, max_body, jnp.max(s_own, axis=0, keepdims=True))
    p_own = jnp.exp(s_own - m)

    def exp_body(n, l):
        off, s = block_scores(n)
        p = jnp.exp(s - m)
        scores[pl.ds(off, MOBA_BLOCK), :] = p
        return l + jnp.sum(p, axis=0, keepdims=True)

    l = lax.fori_loop(0, n_past, exp_body, jnp.sum(p_own, axis=0, keepdims=True))

    acc[...] = weighted_values(p_own, own_v)

    def v_block(n, rows):
        off = pl.multiple_of(n * MOBA_BLOCK, MOBA_BLOCK)
        acc[...] += weighted_values(scores[pl.ds(off, MOBA_BLOCK), :], rows)

    stream(vpool, v_block)

    inv = 1.0 / _lane_to_rows(l)
    for g in range(N_KV_HEADS):
        rows = slice(g * ROWS_PER_KV, (g + 1) * ROWS_PER_KV)
        o_ref[0, g] = acc[rows, g * HEAD_DIM:(g + 1) * HEAD_DIM] * inv[rows]


def moba_sample(page_table, qbd, k_own, v_own, pool_k, pool_v):
    nb, n_pages = page_table.shape
    n_past = n_pages * PAGE_SIZE // MOBA_BLOCK
    page_spec = pl.BlockSpec((1, PAGE_ROWS, HEAD_DIM), lambda b, pt: (b, 0, 0))
    grid_spec = pltpu.PrefetchScalarGridSpec(
        num_scalar_prefetch=1,
        grid=(nb,),
        in_specs=[pl.BlockSpec((1, KV_DIM, LANES), lambda b, pt: (b, 0, 0)), page_spec, page_spec,
                  pl.BlockSpec(memory_space=pl.ANY), pl.BlockSpec(memory_space=pl.ANY)],
        out_specs=pl.BlockSpec((1, N_KV_HEADS, ROWS_PER_KV, HEAD_DIM), lambda b, pt: (b, 0, 0, 0)),
        scratch_shapes=[
            pltpu.VMEM((CHUNK_SLOTS, PAGES_PER_CHUNK * PAGE_ROWS, HEAD_DIM), F32),
            pltpu.SemaphoreType.DMA((CHUNK_SLOTS, PAGES_PER_CHUNK)),
            pltpu.VMEM((n_past * MOBA_BLOCK, LANES), F32),
            pltpu.VMEM((n_past, KV_DIM), F32),
            pltpu.VMEM((n_past, LANES), F32),
            pltpu.VMEM((LANES, KV_DIM), F32),
        ],
    )
    return pl.pallas_call(
        functools.partial(_moba_sample_kernel, n_pages=n_pages),
        grid_spec=grid_spec,
        out_shape=jax.ShapeDtypeStruct((nb, N_KV_HEADS, ROWS_PER_KV, HEAD_DIM), F32),
        compiler_params=_params("arbitrary"),
        name="moba_sample",
    )(page_table.reshape(-1), qbd, k_own, v_own, pool_k, pool_v)


def sample_query_rows(q):
    nb, t, _ = q.shape
    q = q.reshape(nb, t, N_KV_HEADS, GROUP, HEAD_DIM).transpose(0, 2, 3, 1, 4)
    return q.reshape(nb, N_KV_HEADS, GROUP * t, HEAD_DIM)


def block_diag_queries(q_rows):
    nb = q_rows.shape[0]
    eye = jnp.eye(N_KV_HEADS, dtype=q_rows.dtype)
    return jnp.einsum("bgrd,gh->bgdhr", q_rows, eye).reshape(nb, KV_DIM, N_KV_HEADS * ROWS_PER_KV)


def heads_from_rows(o, t):
    nb = o.shape[0]
    o = o.reshape(nb, N_KV_HEADS, GROUP, t, HEAD_DIM).transpose(0, 3, 1, 2, 4)
    return o.reshape(nb * t, N_HEADS * HEAD_DIM)


def own_page(x):
    nb, t, _ = x.shape
    x = x.reshape(nb, t, N_KV_HEADS, HEAD_DIM)
    return jnp.pad(x, ((0, 0), (0, PAGE_SIZE - t), (0, 0), (0, 0)))


def kernel(x_prompt, x_sample, cache_swa_k, cache_swa_v, cache_moba_k, cache_moba_v, page_table, attn_norm, w_qkv, w_o, swa_sinks, ffn1_norm, ffn1_w_gate, ffn1_w_up, ffn1_w_down, ffn2_norm, ffn2_w_gate, ffn2_w_up, ffn2_w_down, final_norm):
    batch, seq, d = x_prompt.shape
    dec_batch, dec_seq, _ = x_sample.shape
    assert dec_seq == DEC_SEQ
    mp = batch * seq
    depth = w_qkv.shape[0]
    q_dim = N_HEADS * HEAD_DIM

    x = jnp.concatenate([x_prompt.reshape(mp, d), x_sample.reshape(dec_batch * dec_seq, d)], axis=0)
    pos = jnp.concatenate([jnp.tile(jnp.arange(seq), batch), jnp.tile(PAST_LEN + jnp.arange(dec_seq), dec_batch)])
    cosf, sinf = rope_tables(pos)

    ffn = ((ffn1_norm, ffn1_w_gate, ffn1_w_up, ffn1_w_down),
           (ffn2_norm, ffn2_w_gate, ffn2_w_up, ffn2_w_down))

    def half_ffn(x, which, layer):
        norm, wg, wu, wd = ffn[which]
        a, wd_bf16 = gate_up(rmsnorm(x, norm[layer], BF16), wg, wu, wd, layer)
        return matmul_residual(a, wd_bf16[None], 0, x, DOWN_TN, DOWN_TK)

    outs = {}
    for i in range(depth):
        a_idx = i // 2
        x = half_ffn(x, 0, i)
        qkv = qkv_rope(rmsnorm(x, attn_norm[i], BF16), w_qkv, i, cosf, sinf)
        qkv_s = qkv[mp:].reshape(dec_batch, dec_seq, -1)
        k_new_p = qkv[:mp, q_dim:q_dim + KV_DIM].reshape(batch, seq, N_KV_HEADS, HEAD_DIM)
        v_new_p = qkv[:mp, q_dim + KV_DIM:].reshape(batch, seq, N_KV_HEADS, HEAD_DIM)
        k_new_s = qkv_s[..., q_dim:q_dim + KV_DIM]
        v_new_s = qkv_s[..., q_dim + KV_DIM:]
        q_rows = sample_query_rows(qkv_s[..., :q_dim])
        if i % 2 == 0:
            o_p = swa_prompt(qkv, swa_sinks[a_idx], batch, seq)
            o_s, nk, nv = swa_sample(swa_sinks[a_idx], q_rows, own_page(k_new_s), own_page(v_new_s),
                                     cache_swa_k[a_idx], cache_swa_v[a_idx])
            outs.setdefault("swa_kp", []).append(k_new_p[:, -WINDOW:])
            outs.setdefault("swa_vp", []).append(v_new_p[:, -WINDOW:])
            outs.setdefault("swa_ks", []).append(nk)
            outs.setdefault("swa_vs", []).append(nv)
        else:
            o_p = moba_prompt(qkv, batch, seq)
            flat = (-1, PAGE_ROWS, HEAD_DIM)
            o_s = moba_sample(page_table, block_diag_queries(q_rows),
                              own_page(k_new_s).reshape(flat), own_page(v_new_s).reshape(flat),
                              cache_moba_k[a_idx].reshape(flat), cache_moba_v[a_idx].reshape(flat))
            outs.setdefault("moba_kp", []).append(k_new_p)
            outs.setdefault("moba_vp", []).append(v_new_p)
            outs.setdefault("moba_ks", []).append(k_new_s.reshape(dec_batch, dec_seq, N_KV_HEADS, HEAD_DIM))
            outs.setdefault("moba_vs", []).append(v_new_s.reshape(dec_batch, dec_seq, N_KV_HEADS, HEAD_DIM))
        o = jnp.concatenate([o_p, heads_from_rows(o_s, dec_seq).astype(BF16)], axis=0)
        x = matmul_residual(o, w_o, i, x, OUT_TN, OUT_TK)
        x = half_ffn(x, 1, i)

    ms = dec_batch * dec_seq
    y_prompt = rmsnorm(x, final_norm, F32, tile=FINAL_TILE, rows=mp).reshape(batch, seq, d)
    y_sample = rmsnorm(x, final_norm, F32, tile=ms, first_tile=mp // ms, rows=ms).reshape(dec_batch, dec_seq, d)
    return (y_prompt, y_sample,
            jnp.stack(outs["swa_kp"]), jnp.stack(outs["swa_vp"]),
            jnp.stack(outs["swa_ks"]), jnp.stack(outs["swa_vs"]),
            jnp.stack(outs["moba_kp"]), jnp.stack(outs["moba_vp"]),
            jnp.stack(outs["moba_ks"]), jnp.stack(outs["moba_vs"]))
```

```python
import functools

import jax
import jax.numpy as jnp
from jax import lax
from jax.experimental import pallas as pl
from jax.experimental.pallas import tpu as pltpu

F32 = jnp.float32
BF16 = jnp.bfloat16

D_MODEL = 4096
N_HEADS = 32
N_KV_HEADS = 8
HEAD_DIM = 128
GROUP = N_HEADS // N_KV_HEADS
KV_DIM = N_KV_HEADS * HEAD_DIM
D_FF = 11008
WINDOW = 128
MOBA_BLOCK = 256
MOBA_TOPK = 3
PAGE_SIZE = 128
PAGE_ROWS = PAGE_SIZE * N_KV_HEADS
PAST_LEN = 16384
DEC_SEQ = 4
ROPE_THETA = 10000.0
RMS_EPS = 1e-6
SCALE = HEAD_DIM ** -0.5
LOG2E = 1.4426950408889634

V7X_VMEM_BYTES = 64 * 1024 * 1024
VMEM_LIMIT = V7X_VMEM_BYTES - 8 * 1024 * 1024
LANES = 128

ROW_TILE = 1040
NORM_TILE = 416
FINAL_TILE = 512
SWA_QT = 256
GATE_ROW_TILE = 2080
FF_TILE = 256
D_FF_PAD = 11264
DOWN_TN = 1024
DOWN_TK = 2816
QKV_TN = 1024
OUT_TN = 1024
OUT_TK = 2048
PAGES_PER_CHUNK = 4
CHUNK_SLOTS = 6
ROWS_PER_KV = GROUP * DEC_SEQ
NEG_INF = float("-inf")
NT_DIMS = (((1,), (1,)), ((), ()))
TN_DIMS = (((0,), (0,)), ((), ()))


def _params(*sem):
    return pltpu.CompilerParams(dimension_semantics=sem, vmem_limit_bytes=VMEM_LIMIT)


def _rms_kernel(x_ref, g_ref, o_ref):
    x = x_ref[...]
    ms = jnp.mean(x * x, axis=-1, keepdims=True)
    o_ref[...] = ((x * lax.rsqrt(ms + RMS_EPS)) * g_ref[...]).astype(o_ref.dtype)


def rmsnorm(x, gain, out_dtype, tile=NORM_TILE, first_tile=0, rows=None):
    d = x.shape[1]
    rows = x.shape[0] if rows is None else rows
    return pl.pallas_call(
        _rms_kernel,
        grid=(rows // tile,),
        in_specs=[pl.BlockSpec((tile, d), lambda i: (first_tile + i, 0)),
                  pl.BlockSpec((1, d), lambda i: (0, 0))],
        out_specs=pl.BlockSpec((tile, d), lambda i: (i, 0)),
        out_shape=jax.ShapeDtypeStruct((rows, d), out_dtype),
        compiler_params=_params("parallel"),
        name="rmsnorm",
    )(x, gain.reshape(1, d))


def _gateup_kernel(h_ref, wg_ref, wu_ref, wd_ref, o_ref, wdb_ref, *, real_tiles):
    i = pl.program_id(0)
    j = pl.program_id(1)

    @pl.when(j < real_tiles)
    def _():
        h = h_ref[...]
        g = jnp.dot(h, wg_ref[...].astype(BF16), preferred_element_type=F32)
        u = jnp.dot(h, wu_ref[...].astype(BF16), preferred_element_type=F32)
        o_ref[...] = (0.5 * g * jax.nn.sigmoid(g) * u).astype(o_ref.dtype)

    @pl.when(j >= real_tiles)
    def _():
        o_ref[...] = jnp.zeros_like(o_ref)

    @pl.when((i == 0) & (j < real_tiles))
    def _():
        wdb_ref[...] = wd_ref[...].astype(BF16)

    @pl.when((i == 0) & (j >= real_tiles))
    def _():
        wdb_ref[...] = jnp.zeros_like(wdb_ref)


def gate_up(h, wg, wu, wd, layer):
    m, d = h.shape
    real_tiles = wg.shape[2] // FF_TILE
    pad_tiles = D_FF_PAD // FF_TILE
    assert pad_tiles == real_tiles + 1

    def real(j):
        return jnp.minimum(j, real_tiles - 1)

    w_spec = pl.BlockSpec((None, d, FF_TILE), lambda i, j: (layer, 0, real(j)))
    wd_spec = pl.BlockSpec((None, FF_TILE, d), lambda i, j: (layer, jnp.where(i == 0, real(j), real_tiles - 1), 0))
    wdb_spec = pl.BlockSpec((FF_TILE, d), lambda i, j: (jnp.where(i == 0, j, real_tiles), 0))
    return pl.pallas_call(
        functools.partial(_gateup_kernel, real_tiles=real_tiles),
        grid=(m // GATE_ROW_TILE, pad_tiles),
        in_specs=[pl.BlockSpec((GATE_ROW_TILE, d), lambda i, j: (i, 0), pipeline_mode=pl.Buffered(1)),
                  w_spec, w_spec, wd_spec],
        out_specs=[pl.BlockSpec((GATE_ROW_TILE, FF_TILE), lambda i, j: (i, j)), wdb_spec],
        out_shape=[jax.ShapeDtypeStruct((m, D_FF_PAD), BF16), jax.ShapeDtypeStruct((D_FF_PAD, d), BF16)],
        compiler_params=_params("arbitrary", "arbitrary"),
        name="gate_up",
    )(h, wg, wu, wd)


def _matmul_residual_kernel(a_ref, w_ref, r_ref, *rest, n_side):
    side_in, o_ref, side_out = rest[:n_side], rest[n_side], rest[n_side + 1:]
    k = pl.program_id(2)

    @pl.when(k == 0)
    def _():
        o_ref[...] = r_ref[...] + jnp.dot(a_ref[...], w_ref[...].astype(BF16), preferred_element_type=F32)

    @pl.when(k > 0)
    def _():
        o_ref[...] += jnp.dot(a_ref[...], w_ref[...].astype(BF16), preferred_element_type=F32)

    for src, dst in zip(side_in, side_out):
        dst[...] = src[...].astype(BF16)


def matmul_residual(a, w, layer, res, tn, tk, side=()):
    m, kdim = a.shape
    n = w.shape[2]
    gi, gj, gk = m // ROW_TILE, n // tn, kdim // tk
    steps = gi * gj * gk

    def step(i, j, k):
        return (i * gj + j) * gk + k

    side_in, side_out, side_shapes = [], [], []
    for sw, sl in side:
        rows, cols = sw.shape[1] // steps, sw.shape[2]
        assert rows * steps == sw.shape[1]
        side_in.append(pl.BlockSpec((None, rows, cols), lambda i, j, k, sl=sl: (sl, step(i, j, k), 0)))
        side_out.append(pl.BlockSpec((rows, cols), lambda i, j, k: (step(i, j, k), 0)))
        side_shapes.append(jax.ShapeDtypeStruct(sw.shape[1:], BF16))
    out = pl.pallas_call(
        functools.partial(_matmul_residual_kernel, n_side=len(side)),
        grid=(gi, gj, gk),
        in_specs=[pl.BlockSpec((ROW_TILE, tk), lambda i, j, k: (i, k)),
                  pl.BlockSpec((None, tk, tn), lambda i, j, k: (layer, k, j)),
                  pl.BlockSpec((ROW_TILE, tn), lambda i, j, k: (i, j))] + side_in,
        out_specs=[pl.BlockSpec((ROW_TILE, tn), lambda i, j, k: (i, j))] + side_out,
        out_shape=[jax.ShapeDtypeStruct((m, n), F32)] + side_shapes,
        compiler_params=_params("arbitrary", "arbitrary", "arbitrary"),
        name="matmul_residual",
    )(a, w, res, *[sw for sw, _ in side])
    return (out[0], *out[1:]) if side else out[0]


def _qkv_kernel(h_ref, w_ref, cos_ref, sin_ref, o_ref, *, rope_tiles):
    acc = jnp.dot(h_ref[...], w_ref[...].astype(BF16), preferred_element_type=F32)
    n = pl.program_id(1)

    @pl.when(n < rope_tiles)
    def _():
        cosf = cos_ref[...]
        sinf = sin_ref[...]
        for c in range(acc.shape[1] // HEAD_DIM):
            x = acc[:, c * HEAD_DIM:(c + 1) * HEAD_DIM]
            o_ref[:, c * HEAD_DIM:(c + 1) * HEAD_DIM] = x * cosf + pltpu.roll(x, HEAD_DIM // 2, 1) * sinf

    @pl.when(n >= rope_tiles)
    def _():
        o_ref[...] = acc


def qkv_rope(h, w, layer, cosf, sinf):
    m, d = h.shape
    n = w.shape[2]
    rope_tiles = (N_HEADS + N_KV_HEADS) * HEAD_DIM // QKV_TN
    return pl.pallas_call(
        functools.partial(_qkv_kernel, rope_tiles=rope_tiles),
        grid=(m // ROW_TILE, n // QKV_TN),
        in_specs=[pl.BlockSpec((ROW_TILE, d), lambda i, j: (i, 0)),
                  pl.BlockSpec((None, d, QKV_TN), lambda i, j: (layer, 0, j)),
                  pl.BlockSpec((ROW_TILE, HEAD_DIM), lambda i, j: (i, 0)),
                  pl.BlockSpec((ROW_TILE, HEAD_DIM), lambda i, j: (i, 0))],
        out_specs=pl.BlockSpec((ROW_TILE, QKV_TN), lambda i, j: (i, j)),
        out_shape=jax.ShapeDtypeStruct((m, n), F32),
        compiler_params=_params("parallel", "arbitrary"),
        name="qkv_rope",
    )(h, w, cosf, sinf)


def rope_tables(pos):
    inv = ROPE_THETA ** (-jnp.arange(0, HEAD_DIM, 2, dtype=F32) / HEAD_DIM)
    ang = pos.astype(F32)[:, None] * inv[None, :]
    cos = jnp.cos(ang)
    sin = jnp.sin(ang)
    return jnp.concatenate([cos, cos], axis=-1), jnp.concatenate([-sin, sin], axis=-1)


def _swa_prompt_kernel(sink_ref, q_ref, kp_ref, kc_ref, vp_ref, vc_ref, o_ref):
    g = pl.program_id(1)
    n = pl.program_id(2)
    q = jnp.concatenate([q_ref[:, j * HEAD_DIM:(j + 1) * HEAD_DIM] for j in range(GROUP)], axis=0).astype(BF16)
    k = jnp.concatenate([kp_ref[...], kc_ref[...]], axis=0).astype(BF16)
    v = jnp.concatenate([vp_ref[...], vc_ref[...]], axis=0).astype(BF16)
    s = lax.dot_general(k, q, NT_DIMS, preferred_element_type=F32) * SCALE
    key = lax.broadcasted_iota(jnp.int32, s.shape, 0)
    query = lax.broadcasted_iota(jnp.int32, s.shape, 1) % SWA_QT
    mask = (key > query) & (key <= query + WINDOW) & ((key >= WINDOW) | (n > 0))
    s = jnp.where(mask, s, NEG_INF)
    head = lax.broadcasted_iota(jnp.int32, (1, s.shape[1]), 1) // SWA_QT
    sink = jnp.zeros((1, s.shape[1]), F32)
    for j in range(GROUP):
        sink = jnp.where(head == j, sink_ref[g * GROUP + j], sink)
    m = jnp.maximum(jnp.max(s, axis=0, keepdims=True), sink)
    p = jnp.exp(s - m)
    denom = jnp.sum(p, axis=0, keepdims=True) + jnp.exp(sink - m)
    out = lax.dot_general(v, p.astype(BF16), TN_DIMS, preferred_element_type=F32) / denom
    for j in range(GROUP):
        o_ref[:, j * HEAD_DIM:(j + 1) * HEAD_DIM] = out[:, j * SWA_QT:(j + 1) * SWA_QT].T.astype(o_ref.dtype)


def swa_prompt(qkv, sinks, batch, seq):
    nt = seq // SWA_QT
    wpt = SWA_QT // WINDOW
    kcol = N_HEADS
    vcol = N_HEADS + N_KV_HEADS
    qw = GROUP * HEAD_DIM

    def prev_window(col):
        return pl.BlockSpec((WINDOW, HEAD_DIM),
                            lambda b, g, n: ((b * nt + n) * wpt - jnp.minimum(n, 1), col + g))

    def tile(col):
        return pl.BlockSpec((SWA_QT, HEAD_DIM), lambda b, g, n: (b * nt + n, col + g))

    return pl.pallas_call(
        _swa_prompt_kernel,
        grid=(batch, N_KV_HEADS, nt),
        in_specs=[
            pl.BlockSpec(memory_space=pltpu.SMEM),
            pl.BlockSpec((SWA_QT, qw), lambda b, g, n: (b * nt + n, g)),
            prev_window(kcol), tile(kcol), prev_window(vcol), tile(vcol),
        ],
        out_specs=pl.BlockSpec((SWA_QT, qw), lambda b, g, n: (b * nt + n, g)),
        out_shape=jax.ShapeDtypeStruct((batch * seq, N_HEADS * HEAD_DIM), BF16),
        compiler_params=_params("parallel", "parallel", "arbitrary"),
        name="swa_prompt",
    )(sinks, qkv, qkv, qkv, qkv, qkv)


def _moba_prompt_kernel(q_ref, k_ref, v_ref, o_ref, means_ref, sel_ref, acc_ref, *, nblk):
    c = pl.program_id(2)
    nq = GROUP * MOBA_BLOCK

    @pl.when(c == 0)
    def _():
        for n in range(nblk):
            kb = k_ref[n * MOBA_BLOCK:(n + 1) * MOBA_BLOCK, :]
            means_ref[n:n + 1, :] = jnp.sum(kb, axis=0, keepdims=True) / MOBA_BLOCK

    qf = jnp.concatenate([q_ref[:, j * HEAD_DIM:(j + 1) * HEAD_DIM] for j in range(GROUP)], axis=0)
    q = (qf * (SCALE * LOG2E)).astype(BF16)

    gate = lax.dot_general(means_ref[...], qf, NT_DIMS,
                           precision=lax.Precision.HIGHEST, preferred_element_type=F32)
    blk = lax.broadcasted_iota(jnp.int32, gate.shape, 0)
    gate = jnp.where(blk < c, gate, NEG_INF)
    for n in range(nblk):
        gate_n = gate[n:n + 1, :]
        ahead = (gate > gate_n) | ((gate == gate_n) & (blk < n))
        rank = jnp.sum(jnp.where(ahead, 1.0, 0.0), axis=0, keepdims=True)
        sel_ref[n:n + 1, :] = jnp.where(rank < MOBA_TOPK, 1.0, 0.0)

    def block_scores(off):
        kb = k_ref[pl.ds(off, MOBA_BLOCK), :].astype(BF16)
        return lax.dot_general(kb, q, NT_DIMS, preferred_element_type=F32)

    def weighted_values(off, p):
        vb = v_ref[pl.ds(off, MOBA_BLOCK), :].astype(BF16)
        return lax.dot_general(vb, p.astype(BF16), TN_DIMS, preferred_element_type=F32)

    own = pl.multiple_of(c * MOBA_BLOCK, MOBA_BLOCK)
    s = block_scores(own)
    key = lax.broadcasted_iota(jnp.int32, s.shape, 0)
    query = lax.broadcasted_iota(jnp.int32, s.shape, 1) % MOBA_BLOCK
    s = jnp.where(key <= query, s, NEG_INF)
    m0 = jnp.max(s, axis=0, keepdims=True)
    p = jnp.exp2(s - m0)
    l0 = jnp.sum(p, axis=0, keepdims=True)
    acc_ref[...] = weighted_values(own, p)

    def past_block(n, carry):
        m, l = carry
        off = pl.multiple_of(n * MOBA_BLOCK, MOBA_BLOCK)
        sb = jnp.where(sel_ref[pl.ds(n, 1), :] > 0.0, block_scores(off), NEG_INF)
        m_new = jnp.maximum(m, jnp.max(sb, axis=0, keepdims=True))
        alpha = jnp.exp2(m - m_new)
        pb = jnp.exp2(sb - m_new)
        acc_ref[...] = alpha * acc_ref[...] + weighted_values(off, pb)
        return m_new, alpha * l + jnp.sum(pb, axis=0, keepdims=True)

    _, l = lax.fori_loop(0, c, past_block, (m0, l0))
    out = acc_ref[...] / l
    for j in range(GROUP):
        o_ref[:, j * HEAD_DIM:(j + 1) * HEAD_DIM] = out[:, j * MOBA_BLOCK:(j + 1) * MOBA_BLOCK].T.astype(o_ref.dtype)


def moba_prompt(qkv, batch, seq):
    nblk = seq // MOBA_BLOCK
    kcol = N_HEADS
    vcol = N_HEADS + N_KV_HEADS
    qw = GROUP * HEAD_DIM
    nq = GROUP * MOBA_BLOCK
    return pl.pallas_call(
        functools.partial(_moba_prompt_kernel, nblk=nblk),
        grid=(batch, N_KV_HEADS, nblk),
        in_specs=[
            pl.BlockSpec((MOBA_BLOCK, qw), lambda b, g, c: (b * nblk + c, g)),
            pl.BlockSpec((seq, HEAD_DIM), lambda b, g, c: (b, kcol + g)),
            pl.BlockSpec((seq, HEAD_DIM), lambda b, g, c: (b, vcol + g)),
        ],
        out_specs=pl.BlockSpec((MOBA_BLOCK, qw), lambda b, g, c: (b * nblk + c, g)),
        out_shape=jax.ShapeDtypeStruct((batch * seq, N_HEADS * HEAD_DIM), BF16),
        scratch_shapes=[pltpu.VMEM((nblk, HEAD_DIM), F32),
                        pltpu.VMEM((nblk, nq), F32),
                        pltpu.VMEM((HEAD_DIM, nq), F32)],
        compiler_params=_params("parallel", "parallel", "arbitrary"),
        name="moba_prompt",
    )(qkv, qkv, qkv)


def _swa_sample_kernel(sink_ref, q_ref, kn_ref, vn_ref, ck_ref, cv_ref, o_ref, nk_ref, nv_ref):
    keep = WINDOW - DEC_SEQ
    nk_ref[0, 0:keep] = ck_ref[0, DEC_SEQ:WINDOW]
    nk_ref[0, keep:WINDOW] = kn_ref[0, 0:DEC_SEQ]
    nv_ref[0, 0:keep] = cv_ref[0, DEC_SEQ:WINDOW]
    nv_ref[0, keep:WINDOW] = vn_ref[0, 0:DEC_SEQ]

    t = lax.broadcasted_iota(jnp.int32, (ROWS_PER_KV, 2 * WINDOW), 0) % DEC_SEQ
    col = lax.broadcasted_iota(jnp.int32, (ROWS_PER_KV, 2 * WINDOW), 1)
    mask = ((col < WINDOW) & (col > t)) | ((col >= WINDOW) & (col - WINDOW <= t))
    row = lax.broadcasted_iota(jnp.int32, (ROWS_PER_KV, 1), 0)
    for g in range(N_KV_HEADS):
        q = q_ref[0, g].astype(BF16)
        k = jnp.concatenate([ck_ref[0, :, g, :], kn_ref[0, :, g, :]], axis=0).astype(BF16)
        v = jnp.concatenate([cv_ref[0, :, g, :], vn_ref[0, :, g, :]], axis=0).astype(BF16)
        s = lax.dot_general(q, k, NT_DIMS, preferred_element_type=F32) * SCALE
        s = jnp.where(mask, s, NEG_INF)
        sink = jnp.zeros((ROWS_PER_KV, 1), F32)
        for j in range(GROUP):
            sink = jnp.where(row // DEC_SEQ == j, sink_ref[g * GROUP + j], sink)
        m = jnp.maximum(jnp.max(s, axis=-1, keepdims=True), sink)
        p = jnp.exp(s - m)
        denom = jnp.sum(p, axis=-1, keepdims=True) + jnp.exp(sink - m)
        o_ref[0, g] = jnp.dot(p.astype(BF16), v, preferred_element_type=F32) / denom


def swa_sample(sinks, q_rows, k_own, v_own, cache_k, cache_v):
    nb = q_rows.shape[0]
    page_spec = pl.BlockSpec((1, PAGE_SIZE, N_KV_HEADS, HEAD_DIM), lambda b: (b, 0, 0, 0))
    q_spec = pl.BlockSpec((1, N_KV_HEADS, ROWS_PER_KV, HEAD_DIM), lambda b: (b, 0, 0, 0))
    cache_shape = jax.ShapeDtypeStruct((nb, WINDOW, N_KV_HEADS, HEAD_DIM), F32)
    return pl.pallas_call(
        _swa_sample_kernel,
        grid=(nb,),
        in_specs=[pl.BlockSpec(memory_space=pltpu.SMEM), q_spec, page_spec, page_spec, page_spec, page_spec],
        out_specs=[q_spec, page_spec, page_spec],
        out_shape=[jax.ShapeDtypeStruct(q_rows.shape, F32), cache_shape, cache_shape],
        compiler_params=_params("parallel"),
        name="swa_sample",
    )(sinks, q_rows, k_own, v_own, cache_k, cache_v)


def _head_concat(flat_ref, first_token, n_tokens):
    return jnp.concatenate(
        [flat_ref[pl.ds(first_token * N_KV_HEADS + g, n_tokens, stride=N_KV_HEADS), :]
         for g in range(N_KV_HEADS)], axis=1)


def _lane_to_rows(row_vec):
    i = lax.broadcasted_iota(jnp.int32, (LANES, LANES), 0)
    j = lax.broadcasted_iota(jnp.int32, (LANES, LANES), 1)
    return jnp.sum(jnp.where(i == j, row_vec, 0.0), axis=1, keepdims=True)


def _moba_sample_kernel(pt_ref, qbd_ref, kn_ref, vn_ref, kpool, vpool, o_ref,
                        buf, sem, scores, means, sel, acc, *, n_pages):
    b = pl.program_id(0)
    chunk_tokens = PAGES_PER_CHUNK * PAGE_SIZE
    n_chunks = n_pages // PAGES_PER_CHUNK
    blocks_per_chunk = chunk_tokens // MOBA_BLOCK
    n_past = n_pages * PAGE_SIZE // MOBA_BLOCK

    def chunk_copy(pool, ci, p):
        page = pt_ref[b * n_pages + ci * PAGES_PER_CHUNK + p]
        slot = ci % CHUNK_SLOTS
        return pltpu.make_async_copy(pool.at[page], buf.at[slot, pl.ds(p * PAGE_ROWS, PAGE_ROWS)],
                                     sem.at[slot, p])

    def start_chunk(pool, ci):
        for p in range(PAGES_PER_CHUNK):
            chunk_copy(pool, ci, p).start()

    def wait_chunk(pool, ci):
        for p in range(PAGES_PER_CHUNK):
            chunk_copy(pool, ci, p).wait()

    def prime(pool):
        for ci in range(CHUNK_SLOTS - 1):
            start_chunk(pool, ci)

    def stream(pool, per_block):
        def body(ci, carry):
            @pl.when(ci + CHUNK_SLOTS - 1 < n_chunks)
            def _():
                start_chunk(pool, ci + CHUNK_SLOTS - 1)

            wait_chunk(pool, ci)
            flat = buf.at[ci % CHUNK_SLOTS]
            for h in range(blocks_per_chunk):
                per_block(ci * blocks_per_chunk + h, _head_concat(flat, h * MOBA_BLOCK, MOBA_BLOCK))
            return carry

        lax.fori_loop(0, n_chunks, body, 0)

    def scores_of(rows, qbd):
        return jnp.dot(rows.astype(BF16), qbd, preferred_element_type=F32) * SCALE

    def weighted_values(p, rows):
        return lax.dot_general(p.astype(BF16), rows.astype(BF16), TN_DIMS, preferred_element_type=F32)

    qbd_f32 = qbd_ref[0]
    qbd = qbd_f32.astype(BF16)

    def k_block(n, rows):
        means[pl.ds(n, 1), :] = jnp.sum(rows, axis=0, keepdims=True) / MOBA_BLOCK
        off = pl.multiple_of(n * MOBA_BLOCK, MOBA_BLOCK)
        scores[pl.ds(off, MOBA_BLOCK), :] = scores_of(rows, qbd)

    prime(kpool)
    stream(kpool, k_block)
    prime(vpool)

    gate = jnp.dot(means[...], qbd_f32, precision=lax.Precision.HIGHEST, preferred_element_type=F32)
    blk = lax.broadcasted_iota(jnp.int32, gate.shape, 0).astype(F32)
    chosen = jnp.zeros(gate.shape, F32)
    for _ in range(MOBA_TOPK):
        best = jnp.max(gate, axis=0, keepdims=True)
        first = jnp.min(jnp.where(gate == best, blk, float(n_past)), axis=0, keepdims=True)
        hit = blk == first
        chosen = jnp.where(hit, 1.0, chosen)
        gate = jnp.where(hit, NEG_INF, gate)
    sel[...] = chosen

    own_k = _head_concat(kn_ref.at[0], 0, PAGE_SIZE)
    own_v = _head_concat(vn_ref.at[0], 0, PAGE_SIZE)
    s_own = scores_of(own_k, qbd)
    row = lax.broadcasted_iota(jnp.int32, s_own.shape, 0)
    t = lax.broadcasted_iota(jnp.int32, s_own.shape, 1) % DEC_SEQ
    s_own = jnp.where(row <= t, s_own, NEG_INF)

    def block_scores(n):
        off = pl.multiple_of(n * MOBA_BLOCK, MOBA_BLOCK)
        return off, jnp.where(sel[pl.ds(n, 1), :] > 0.0, scores[pl.ds(off, MOBA_BLOCK), :], NEG_INF)

    def max_body(n, m):
        return jnp.maximum(m, jnp.max(block_scores(n)[1], axis=0, keepdims=True))

    m = lax.fori_loop(0, n_past, max_body, jnp.max(s_own, axis=0, keepdims=True))
    p_own = jnp.exp(s_own - m)

    def exp_body(n, l):
        off, s = block_scores(n)
        p = jnp.exp(s - m)
        scores[pl.ds(off, MOBA_BLOCK), :] = p
        return l + jnp.sum(p, axis=0, keepdims=True)

    l = lax.fori_loop(0, n_past, exp_body, jnp.sum(p_own, axis=0, keepdims=True))

    acc[...] = weighted_values(p_own, own_v)

    def v_block(n, rows):
        off = pl.multiple_of(n * MOBA_BLOCK, MOBA_BLOCK)
        acc[...] += weighted_values(scores[pl.ds(off, MOBA_BLOCK), :], rows)

    stream(vpool, v_block)

    inv = 1.0 / _lane_to_rows(l)
    for g in range(N_KV_HEADS):
        rows = slice(g * ROWS_PER_KV, (g + 1) * ROWS_PER_KV)
        o_ref[0, g] = acc[rows, g * HEAD_DIM:(g + 1) * HEAD_DIM] * inv[rows]


def moba_sample(page_table, qbd, k_own, v_own, pool_k, pool_v):
    nb, n_pages = page_table.shape
    n_past = n_pages * PAGE_SIZE // MOBA_BLOCK
    page_spec = pl.BlockSpec((1, PAGE_ROWS, HEAD_DIM), lambda b, pt: (b, 0, 0))
    grid_spec = pltpu.PrefetchScalarGridSpec(
        num_scalar_prefetch=1,
        grid=(nb,),
        in_specs=[pl.BlockSpec((1, KV_DIM, LANES), lambda b, pt: (b, 0, 0)), page_spec, page_spec,
                  pl.BlockSpec(memory_space=pl.ANY), pl.BlockSpec(memory_space=pl.ANY)],
        out_specs=pl.BlockSpec((1, N_KV_HEADS, ROWS_PER_KV, HEAD_DIM), lambda b, pt: (b, 0, 0, 0)),
        scratch_shapes=[
            pltpu.VMEM((CHUNK_SLOTS, PAGES_PER_CHUNK * PAGE_ROWS, HEAD_DIM), F32),
            pltpu.SemaphoreType.DMA((CHUNK_SLOTS, PAGES_PER_CHUNK)),
            pltpu.VMEM((n_past * MOBA_BLOCK, LANES), F32),
            pltpu.VMEM((n_past, KV_DIM), F32),
            pltpu.VMEM((n_past, LANES), F32),
            pltpu.VMEM((LANES, KV_DIM), F32),
        ],
    )
    return pl.pallas_call(
        functools.partial(_moba_sample_kernel, n_pages=n_pages),
        grid_spec=grid_spec,
        out_shape=jax.ShapeDtypeStruct((nb, N_KV_HEADS, ROWS_PER_KV, HEAD_DIM), F32),
        compiler_params=_params("arbitrary"),
        name="moba_sample",
    )(page_table.reshape(-1), qbd, k_own, v_own, pool_k, pool_v)


def sample_query_rows(q):
    nb, t, _ = q.shape
    q = q.reshape(nb, t, N_KV_HEADS, GROUP, HEAD_DIM).transpose(0, 2, 3, 1, 4)
    return q.reshape(nb, N_KV_HEADS, GROUP * t, HEAD_DIM)


def block_diag_queries(q_rows):
    nb = q_rows.shape[0]
    eye = jnp.eye(N_KV_HEADS, dtype=q_rows.dtype)
    return jnp.einsum("bgrd,gh->bgdhr", q_rows, eye).reshape(nb, KV_DIM, N_KV_HEADS * ROWS_PER_KV)


def heads_from_rows(o, t):
    nb = o.shape[0]
    o = o.reshape(nb, N_KV_HEADS, GROUP, t, HEAD_DIM).transpose(0, 3, 1, 2, 4)
    return o.reshape(nb * t, N_HEADS * HEAD_DIM)


def own_page(x):
    nb, t, _ = x.shape
    x = x.reshape(nb, t, N_KV_HEADS, HEAD_DIM)
    return jnp.pad(x, ((0, 0), (0, PAGE_SIZE - t), (0, 0), (0, 0)))


def kernel(x_prompt, x_sample, cache_swa_k, cache_swa_v, cache_moba_k, cache_moba_v, page_table, attn_norm, w_qkv, w_o, swa_sinks, ffn1_norm, ffn1_w_gate, ffn1_w_up, ffn1_w_down, ffn2_norm, ffn2_w_gate, ffn2_w_up, ffn2_w_down, final_norm):
    batch, seq, d = x_prompt.shape
    dec_batch, dec_seq, _ = x_sample.shape
    assert dec_seq == DEC_SEQ
    mp = batch * seq
    depth = w_qkv.shape[0]
    q_dim = N_HEADS * HEAD_DIM

    x = jnp.concatenate([x_prompt.reshape(mp, d), x_sample.reshape(dec_batch * dec_seq, d)], axis=0)
    pos = jnp.concatenate([jnp.tile(jnp.arange(seq), batch), jnp.tile(PAST_LEN + jnp.arange(dec_seq), dec_batch)])
    cosf, sinf = rope_tables(pos)

    ffn = ((ffn1_norm, ffn1_w_gate, ffn1_w_up, ffn1_w_down),
           (ffn2_norm, ffn2_w_gate, ffn2_w_up, ffn2_w_down))

    def half_ffn(x, which, layer, side=()):
        norm, wg, wu, wd = ffn[which]
        a, wd_bf16 = gate_up(rmsnorm(x, norm[layer], BF16), wg, wu, wd, layer)
        return matmul_residual(a, wd_bf16[None], 0, x, DOWN_TN, DOWN_TK, side=side)

    outs = {}
    for i in range(depth):
        a_idx = i // 2
        x, w_qkv_bf16, w_o_bf16 = half_ffn(x, 0, i, side=((w_qkv, i), (w_o, i)))
        qkv = qkv_rope(rmsnorm(x, attn_norm[i], BF16), w_qkv_bf16[None], 0, cosf, sinf)
        qkv_s = qkv[mp:].reshape(dec_batch, dec_seq, -1)
        k_new_p = qkv[:mp, q_dim:q_dim + KV_DIM].reshape(batch, seq, N_KV_HEADS, HEAD_DIM)
        v_new_p = qkv[:mp, q_dim + KV_DIM:].reshape(batch, seq, N_KV_HEADS, HEAD_DIM)
        k_new_s = qkv_s[..., q_dim:q_dim + KV_DIM]
        v_new_s = qkv_s[..., q_dim + KV_DIM:]
        q_rows = sample_query_rows(qkv_s[..., :q_dim])
        if i % 2 == 0:
            o_p = swa_prompt(qkv, swa_sinks[a_idx], batch, seq)
            o_s, nk, nv = swa_sample(swa_sinks[a_idx], q_rows, own_page(k_new_s), own_page(v_new_s),
                                     cache_swa_k[a_idx], cache_swa_v[a_idx])
            outs.setdefault("swa_kp", []).append(k_new_p[:, -WINDOW:])
            outs.setdefault("swa_vp", []).append(v_new_p[:, -WINDOW:])
            outs.setdefault("swa_ks", []).append(nk)
            outs.setdefault("swa_vs", []).append(nv)
        else:
            o_p = moba_prompt(qkv, batch, seq)
            flat = (-1, PAGE_ROWS, HEAD_DIM)
            o_s = moba_sample(page_table, block_diag_queries(q_rows),
                              own_page(k_new_s).reshape(flat), own_page(v_new_s).reshape(flat),
                              cache_moba_k[a_idx].reshape(flat), cache_moba_v[a_idx].reshape(flat))
            outs.setdefault("moba_kp", []).append(k_new_p)
            outs.setdefault("moba_vp", []).append(v_new_p)
            outs.setdefault("moba_ks", []).append(k_new_s.reshape(dec_batch, dec_seq, N_KV_HEADS, HEAD_DIM))
            outs.setdefault("moba_vs", []).append(v_new_s.reshape(dec_batch, dec_seq, N_KV_HEADS, HEAD_DIM))
        o = jnp.concatenate([o_p, heads_from_rows(o_s, dec_seq).astype(BF16)], axis=0)
        x = matmul_residual(o, w_o_bf16[None], 0, x, OUT_TN, OUT_TK)
        x = half_ffn(x, 1, i)

    ms = dec_batch * dec_seq
    y_prompt = rmsnorm(x, final_norm, F32, tile=FINAL_TILE, rows=mp).reshape(batch, seq, d)
    y_sample = rmsnorm(x, final_norm, F32, tile=ms, first_tile=mp // ms, rows=ms).reshape(dec_batch, dec_seq, d)
    return (y_prompt, y_sample,
            jnp.stack(outs["swa_kp"]), jnp.stack(outs["swa_vp"]),
            jnp.stack(outs["swa_ks"]), jnp.stack(outs["swa_vs"]),
            jnp.stack(outs["moba_kp"]), jnp.stack(outs["moba_vp"]),
            jnp.stack(outs["moba_ks"]), jnp.stack(outs["moba_vs"]))
```

```python
import functools

import jax
import jax.numpy as jnp
from jax import lax
from jax.experimental import pallas as pl
from jax.experimental.pallas import tpu as pltpu

F32 = jnp.float32
BF16 = jnp.bfloat16

D_MODEL = 4096
N_HEADS = 32
N_KV_HEADS = 8
HEAD_DIM = 128
GROUP = N_HEADS // N_KV_HEADS
KV_DIM = N_KV_HEADS * HEAD_DIM
D_FF = 11008
WINDOW = 128
MOBA_BLOCK = 256
MOBA_TOPK = 3
PAGE_SIZE = 128
PAGE_ROWS = PAGE_SIZE * N_KV_HEADS
PAST_LEN = 16384
DEC_SEQ = 4
ROPE_THETA = 10000.0
RMS_EPS = 1e-6
SCALE = HEAD_DIM ** -0.5
LOG2E = 1.4426950408889634

V7X_VMEM_BYTES = 64 * 1024 * 1024
VMEM_LIMIT = V7X_VMEM_BYTES - 8 * 1024 * 1024
LANES = 128

ROW_TILE = 1040
NORM_TILE = 416
FINAL_TILE = 512
SWA_QT = 256
PROMPT_KV_HEADS = 4
GATE_ROW_TILE = 2080
FF_TILE = 256
D_FF_PAD = 11264
DOWN_TN = 1024
DOWN_TK = 2816
QKV_TN = 1024
OUT_TN = 1024
OUT_TK = 2048
PAGES_PER_CHUNK = 4
CHUNK_SLOTS = 6
ROWS_PER_KV = GROUP * DEC_SEQ
NEG_INF = float("-inf")
NT_DIMS = (((1,), (1,)), ((), ()))
TN_DIMS = (((0,), (0,)), ((), ()))


def _params(*sem):
    return pltpu.CompilerParams(dimension_semantics=sem, vmem_limit_bytes=VMEM_LIMIT)


def _rms_kernel(x_ref, g_ref, o_ref):
    x = x_ref[...]
    ms = jnp.mean(x * x, axis=-1, keepdims=True)
    o_ref[...] = ((x * lax.rsqrt(ms + RMS_EPS)) * g_ref[...]).astype(o_ref.dtype)


def rmsnorm(x, gain, out_dtype, tile=NORM_TILE, first_tile=0, rows=None):
    d = x.shape[1]
    rows = x.shape[0] if rows is None else rows
    return pl.pallas_call(
        _rms_kernel,
        grid=(rows // tile,),
        in_specs=[pl.BlockSpec((tile, d), lambda i: (first_tile + i, 0)),
                  pl.BlockSpec((1, d), lambda i: (0, 0))],
        out_specs=pl.BlockSpec((tile, d), lambda i: (i, 0)),
        out_shape=jax.ShapeDtypeStruct((rows, d), out_dtype),
        compiler_params=_params("parallel"),
        name="rmsnorm",
    )(x, gain.reshape(1, d))


def _gateup_kernel(h_ref, wg_ref, wu_ref, wd_ref, o_ref, wdb_ref, *, real_tiles):
    i = pl.program_id(0)
    j = pl.program_id(1)

    @pl.when(j < real_tiles)
    def _():
        h = h_ref[...]
        g = jnp.dot(h, wg_ref[...].astype(BF16), preferred_element_type=F32)
        u = jnp.dot(h, wu_ref[...].astype(BF16), preferred_element_type=F32)
        o_ref[...] = (0.5 * g * jax.nn.sigmoid(g) * u).astype(o_ref.dtype)

    @pl.when(j >= real_tiles)
    def _():
        o_ref[...] = jnp.zeros_like(o_ref)

    @pl.when((i == 0) & (j < real_tiles))
    def _():
        wdb_ref[...] = wd_ref[...].astype(BF16)

    @pl.when((i == 0) & (j >= real_tiles))
    def _():
        wdb_ref[...] = jnp.zeros_like(wdb_ref)


def gate_up(h, wg, wu, wd, layer):
    m, d = h.shape
    real_tiles = wg.shape[2] // FF_TILE
    pad_tiles = D_FF_PAD // FF_TILE
    assert pad_tiles == real_tiles + 1

    def real(j):
        return jnp.minimum(j, real_tiles - 1)

    w_spec = pl.BlockSpec((None, d, FF_TILE), lambda i, j: (layer, 0, real(j)))
    wd_spec = pl.BlockSpec((None, FF_TILE, d), lambda i, j: (layer, jnp.where(i == 0, real(j), real_tiles - 1), 0))
    wdb_spec = pl.BlockSpec((FF_TILE, d), lambda i, j: (jnp.where(i == 0, j, real_tiles), 0))
    return pl.pallas_call(
        functools.partial(_gateup_kernel, real_tiles=real_tiles),
        grid=(m // GATE_ROW_TILE, pad_tiles),
        in_specs=[pl.BlockSpec((GATE_ROW_TILE, d), lambda i, j: (i, 0), pipeline_mode=pl.Buffered(1)),
                  w_spec, w_spec, wd_spec],
        out_specs=[pl.BlockSpec((GATE_ROW_TILE, FF_TILE), lambda i, j: (i, j)), wdb_spec],
        out_shape=[jax.ShapeDtypeStruct((m, D_FF_PAD), BF16), jax.ShapeDtypeStruct((D_FF_PAD, d), BF16)],
        compiler_params=_params("arbitrary", "arbitrary"),
        name="gate_up",
    )(h, wg, wu, wd)


def _matmul_residual_kernel(a_ref, w_ref, r_ref, *rest, n_side):
    side_in, o_ref, side_out = rest[:n_side], rest[n_side], rest[n_side + 1:]
    k = pl.program_id(2)

    @pl.when(k == 0)
    def _():
        o_ref[...] = r_ref[...] + jnp.dot(a_ref[...], w_ref[...].astype(BF16), preferred_element_type=F32)

    @pl.when(k > 0)
    def _():
        o_ref[...] += jnp.dot(a_ref[...], w_ref[...].astype(BF16), preferred_element_type=F32)

    for src, dst in zip(side_in, side_out):
        dst[...] = src[...].astype(BF16)


def matmul_residual(a, w, layer, res, tn, tk, side=()):
    m, kdim = a.shape
    n = w.shape[2]
    gi, gj, gk = m // ROW_TILE, n // tn, kdim // tk
    steps = gi * gj * gk

    def step(i, j, k):
        return (i * gj + j) * gk + k

    side_in, side_out, side_shapes = [], [], []
    for sw, sl in side:
        rows, cols = sw.shape[1] // steps, sw.shape[2]
        assert rows * steps == sw.shape[1]
        side_in.append(pl.BlockSpec((None, rows, cols), lambda i, j, k, sl=sl: (sl, step(i, j, k), 0)))
        side_out.append(pl.BlockSpec((rows, cols), lambda i, j, k: (step(i, j, k), 0)))
        side_shapes.append(jax.ShapeDtypeStruct(sw.shape[1:], BF16))
    out = pl.pallas_call(
        functools.partial(_matmul_residual_kernel, n_side=len(side)),
        grid=(gi, gj, gk),
        in_specs=[pl.BlockSpec((ROW_TILE, tk), lambda i, j, k: (i, k)),
                  pl.BlockSpec((None, tk, tn), lambda i, j, k: (layer, k, j)),
                  pl.BlockSpec((ROW_TILE, tn), lambda i, j, k: (i, j))] + side_in,
        out_specs=[pl.BlockSpec((ROW_TILE, tn), lambda i, j, k: (i, j))] + side_out,
        out_shape=[jax.ShapeDtypeStruct((m, n), F32)] + side_shapes,
        compiler_params=_params("arbitrary", "arbitrary", "arbitrary"),
        name="matmul_residual",
    )(a, w, res, *[sw for sw, _ in side])
    return (out[0], *out[1:]) if side else out[0]


def _qkv_kernel(h_ref, w_ref, cos_ref, sin_ref, o_ref, *, rope_tiles):
    acc = jnp.dot(h_ref[...], w_ref[...].astype(BF16), preferred_element_type=F32)
    n = pl.program_id(1)

    @pl.when(n < rope_tiles)
    def _():
        cosf = cos_ref[...]
        sinf = sin_ref[...]
        for c in range(acc.shape[1] // HEAD_DIM):
            x = acc[:, c * HEAD_DIM:(c + 1) * HEAD_DIM]
            o_ref[:, c * HEAD_DIM:(c + 1) * HEAD_DIM] = x * cosf + pltpu.roll(x, HEAD_DIM // 2, 1) * sinf

    @pl.when(n >= rope_tiles)
    def _():
        o_ref[...] = acc


def qkv_rope(h, w, layer, cosf, sinf):
    m, d = h.shape
    n = w.shape[2]
    rope_tiles = (N_HEADS + N_KV_HEADS) * HEAD_DIM // QKV_TN
    return pl.pallas_call(
        functools.partial(_qkv_kernel, rope_tiles=rope_tiles),
        grid=(m // ROW_TILE, n // QKV_TN),
        in_specs=[pl.BlockSpec((ROW_TILE, d), lambda i, j: (i, 0)),
                  pl.BlockSpec((None, d, QKV_TN), lambda i, j: (layer, 0, j)),
                  pl.BlockSpec((ROW_TILE, HEAD_DIM), lambda i, j: (i, 0)),
                  pl.BlockSpec((ROW_TILE, HEAD_DIM), lambda i, j: (i, 0))],
        out_specs=pl.BlockSpec((ROW_TILE, QKV_TN), lambda i, j: (i, j)),
        out_shape=jax.ShapeDtypeStruct((m, n), F32),
        compiler_params=_params("parallel", "arbitrary"),
        name="qkv_rope",
    )(h, w, cosf, sinf)


def rope_tables(pos):
    inv = ROPE_THETA ** (-jnp.arange(0, HEAD_DIM, 2, dtype=F32) / HEAD_DIM)
    ang = pos.astype(F32)[:, None] * inv[None, :]
    cos = jnp.cos(ang)
    sin = jnp.sin(ang)
    return jnp.concatenate([cos, cos], axis=-1), jnp.concatenate([-sin, sin], axis=-1)


def _swa_prompt_kernel(sink_ref, q_ref, kp_ref, kc_ref, vp_ref, vc_ref, o_ref):
    g = pl.program_id(1)
    n = pl.program_id(2)
    qw = GROUP * HEAD_DIM
    nq = GROUP * SWA_QT
    nk = WINDOW + SWA_QT
    key = lax.broadcasted_iota(jnp.int32, (nk, nq), 0)
    query = lax.broadcasted_iota(jnp.int32, (nk, nq), 1) % SWA_QT
    mask = (key > query) & (key <= query + WINDOW) & ((key >= WINDOW) | (n > 0))
    head = lax.broadcasted_iota(jnp.int32, (1, nq), 1) // SWA_QT
    for h in range(PROMPT_KV_HEADS):
        cols = slice(h * HEAD_DIM, (h + 1) * HEAD_DIM)
        qf = jnp.concatenate([q_ref[:, h * qw + j * HEAD_DIM:h * qw + (j + 1) * HEAD_DIM] for j in range(GROUP)], axis=0)
        q = (qf * (SCALE * LOG2E)).astype(BF16)
        k = jnp.concatenate([kp_ref[:, cols], kc_ref[:, cols]], axis=0).astype(BF16)
        v = jnp.concatenate([vp_ref[:, cols], vc_ref[:, cols]], axis=0).astype(BF16)
        s = jnp.where(mask, lax.dot_general(k, q, NT_DIMS, preferred_element_type=F32), NEG_INF)
        sink = jnp.zeros((1, nq), F32)
        for j in range(GROUP):
            sink = jnp.where(head == j, sink_ref[(g * PROMPT_KV_HEADS + h) * GROUP + j] * LOG2E, sink)
        m = jnp.maximum(jnp.max(s, axis=0, keepdims=True), sink)
        p = jnp.exp2(s - m)
        denom = jnp.sum(p, axis=0, keepdims=True) + jnp.exp2(sink - m)
        out = lax.dot_general(v, p.astype(BF16), TN_DIMS, preferred_element_type=F32) / denom
        for j in range(GROUP):
            o_ref[:, h * qw + j * HEAD_DIM:h * qw + (j + 1) * HEAD_DIM] = out[:, j * SWA_QT:(j + 1) * SWA_QT].T.astype(o_ref.dtype)


def swa_prompt(qkv, sinks, batch, seq):
    nt = seq // SWA_QT
    wpt = SWA_QT // WINDOW
    kw = PROMPT_KV_HEADS * HEAD_DIM
    kcol = N_HEADS * HEAD_DIM // kw
    vcol = (N_HEADS + N_KV_HEADS) * HEAD_DIM // kw
    qw = PROMPT_KV_HEADS * GROUP * HEAD_DIM

    def prev_window(col):
        return pl.BlockSpec((WINDOW, kw), lambda b, g, n: ((b * nt + n) * wpt - jnp.minimum(n, 1), col + g))

    def tile(col):
        return pl.BlockSpec((SWA_QT, kw), lambda b, g, n: (b * nt + n, col + g))

    return pl.pallas_call(
        _swa_prompt_kernel,
        grid=(batch, N_KV_HEADS // PROMPT_KV_HEADS, nt),
        in_specs=[
            pl.BlockSpec(memory_space=pltpu.SMEM),
            pl.BlockSpec((SWA_QT, qw), lambda b, g, n: (b * nt + n, g)),
            prev_window(kcol), tile(kcol), prev_window(vcol), tile(vcol),
        ],
        out_specs=pl.BlockSpec((SWA_QT, qw), lambda b, g, n: (b * nt + n, g)),
        out_shape=jax.ShapeDtypeStruct((batch * seq, N_HEADS * HEAD_DIM), BF16),
        compiler_params=_params("parallel", "parallel", "arbitrary"),
        name="swa_prompt",
    )(sinks, qkv, qkv, qkv, qkv, qkv)


def _moba_prompt_kernel(q_ref, k_ref, v_ref, o_ref, means_ref, sel_ref, acc_ref, *, nblk):
    c = pl.program_id(2)
    qw = GROUP * HEAD_DIM

    @pl.when(c == 0)
    def _():
        for n in range(nblk):
            kb = k_ref[n * MOBA_BLOCK:(n + 1) * MOBA_BLOCK, :]
            means_ref[n:n + 1, :] = jnp.sum(kb, axis=0, keepdims=True) / MOBA_BLOCK

    qs = []
    for h in range(PROMPT_KV_HEADS):
        qf = jnp.concatenate([q_ref[:, h * qw + j * HEAD_DIM:h * qw + (j + 1) * HEAD_DIM] for j in range(GROUP)], axis=0)
        qs.append((qf * (SCALE * LOG2E)).astype(BF16))
        gate = lax.dot_general(means_ref[:, h * HEAD_DIM:(h + 1) * HEAD_DIM], qf, NT_DIMS,
                               precision=lax.Precision.HIGHEST, preferred_element_type=F32)
        blk = lax.broadcasted_iota(jnp.int32, gate.shape, 0)
        gate = jnp.where(blk < c, gate, NEG_INF)
        for n in range(nblk):
            gate_n = gate[n:n + 1, :]
            ahead = (gate > gate_n) | ((gate == gate_n) & (blk < n))
            rank = jnp.sum(jnp.where(ahead, 1.0, 0.0), axis=0, keepdims=True)
            sel_ref[h, n:n + 1, :] = jnp.where(rank < MOBA_TOPK, 1.0, 0.0)

    def block_scores(h, off):
        kb = k_ref[pl.ds(off, MOBA_BLOCK), h * HEAD_DIM:(h + 1) * HEAD_DIM].astype(BF16)
        return lax.dot_general(kb, qs[h], NT_DIMS, preferred_element_type=F32)

    def weighted_values(h, off, p):
        vb = v_ref[pl.ds(off, MOBA_BLOCK), h * HEAD_DIM:(h + 1) * HEAD_DIM].astype(BF16)
        return lax.dot_general(vb, p.astype(BF16), TN_DIMS, preferred_element_type=F32)

    own = pl.multiple_of(c * MOBA_BLOCK, MOBA_BLOCK)
    init = []
    for h in range(PROMPT_KV_HEADS):
        s = block_scores(h, own)
        key = lax.broadcasted_iota(jnp.int32, s.shape, 0)
        query = lax.broadcasted_iota(jnp.int32, s.shape, 1) % MOBA_BLOCK
        s = jnp.where(key <= query, s, NEG_INF)
        m0 = jnp.max(s, axis=0, keepdims=True)
        p = jnp.exp2(s - m0)
        init.append((m0, jnp.sum(p, axis=0, keepdims=True)))
        acc_ref[h] = weighted_values(h, own, p)

    def past_block(n, carry):
        off = pl.multiple_of(n * MOBA_BLOCK, MOBA_BLOCK)
        out = []
        for h in range(PROMPT_KV_HEADS):
            m, l = carry[h]
            sb = jnp.where(sel_ref[h, pl.ds(n, 1), :] > 0.0, block_scores(h, off), NEG_INF)
            m_new = jnp.maximum(m, jnp.max(sb, axis=0, keepdims=True))
            alpha = jnp.exp2(m - m_new)
            pb = jnp.exp2(sb - m_new)
            acc_ref[h] = alpha * acc_ref[h] + weighted_values(h, off, pb)
            out.append((m_new, alpha * l + jnp.sum(pb, axis=0, keepdims=True)))
        return tuple(out)

    final = lax.fori_loop(0, c, past_block, tuple(init))
    for h in range(PROMPT_KV_HEADS):
        out = acc_ref[h] / final[h][1]
        for j in range(GROUP):
            o_ref[:, h * qw + j * HEAD_DIM:h * qw + (j + 1) * HEAD_DIM] = out[:, j * MOBA_BLOCK:(j + 1) * MOBA_BLOCK].T.astype(o_ref.dtype)


def moba_prompt(qkv, batch, seq):
    nblk = seq // MOBA_BLOCK
    kw = PROMPT_KV_HEADS * HEAD_DIM
    kcol = N_HEADS * HEAD_DIM // kw
    vcol = (N_HEADS + N_KV_HEADS) * HEAD_DIM // kw
    qw = PROMPT_KV_HEADS * GROUP * HEAD_DIM
    nq = GROUP * MOBA_BLOCK
    return pl.pallas_call(
        functools.partial(_moba_prompt_kernel, nblk=nblk),
        grid=(batch, N_KV_HEADS // PROMPT_KV_HEADS, nblk),
        in_specs=[
            pl.BlockSpec((MOBA_BLOCK, qw), lambda b, g, c: (b * nblk + c, g)),
            pl.BlockSpec((seq, kw), lambda b, g, c: (b, kcol + g)),
            pl.BlockSpec((seq, kw), lambda b, g, c: (b, vcol + g)),
        ],
        out_specs=pl.BlockSpec((MOBA_BLOCK, qw), lambda b, g, c: (b * nblk + c, g)),
        out_shape=jax.ShapeDtypeStruct((batch * seq, N_HEADS * HEAD_DIM), BF16),
        scratch_shapes=[pltpu.VMEM((nblk, kw), F32),
                        pltpu.VMEM((PROMPT_KV_HEADS, nblk, nq), F32),
                        pltpu.VMEM((PROMPT_KV_HEADS, HEAD_DIM, nq), F32)],
        compiler_params=_params("parallel", "parallel", "arbitrary"),
        name="moba_prompt",
    )(qkv, qkv, qkv)


def _swa_sample_kernel(sink_ref, q_ref, kn_ref, vn_ref, ck_ref, cv_ref, o_ref, nk_ref, nv_ref):
    keep = WINDOW - DEC_SEQ
    nk_ref[0, 0:keep] = ck_ref[0, DEC_SEQ:WINDOW]
    nk_ref[0, keep:WINDOW] = kn_ref[0, 0:DEC_SEQ]
    nv_ref[0, 0:keep] = cv_ref[0, DEC_SEQ:WINDOW]
    nv_ref[0, keep:WINDOW] = vn_ref[0, 0:DEC_SEQ]

    t = lax.broadcasted_iota(jnp.int32, (ROWS_PER_KV, 2 * WINDOW), 0) % DEC_SEQ
    col = lax.broadcasted_iota(jnp.int32, (ROWS_PER_KV, 2 * WINDOW), 1)
    mask = ((col < WINDOW) & (col > t)) | ((col >= WINDOW) & (col - WINDOW <= t))
    row = lax.broadcasted_iota(jnp.int32, (ROWS_PER_KV, 1), 0)
    for g in range(N_KV_HEADS):
        q = q_ref[0, g].astype(BF16)
        k = jnp.concatenate([ck_ref[0, :, g, :], kn_ref[0, :, g, :]], axis=0).astype(BF16)
        v = jnp.concatenate([cv_ref[0, :, g, :], vn_ref[0, :, g, :]], axis=0).astype(BF16)
        s = lax.dot_general(q, k, NT_DIMS, preferred_element_type=F32) * SCALE
        s = jnp.where(mask, s, NEG_INF)
        sink = jnp.zeros((ROWS_PER_KV, 1), F32)
        for j in range(GROUP):
            sink = jnp.where(row // DEC_SEQ == j, sink_ref[g * GROUP + j], sink)
        m = jnp.maximum(jnp.max(s, axis=-1, keepdims=True), sink)
        p = jnp.exp(s - m)
        denom = jnp.sum(p, axis=-1, keepdims=True) + jnp.exp(sink - m)
        o_ref[0, g] = jnp.dot(p.astype(BF16), v, preferred_element_type=F32) / denom


def swa_sample(sinks, q_rows, k_own, v_own, cache_k, cache_v):
    nb = q_rows.shape[0]
    page_spec = pl.BlockSpec((1, PAGE_SIZE, N_KV_HEADS, HEAD_DIM), lambda b: (b, 0, 0, 0))
    q_spec = pl.BlockSpec((1, N_KV_HEADS, ROWS_PER_KV, HEAD_DIM), lambda b: (b, 0, 0, 0))
    cache_shape = jax.ShapeDtypeStruct((nb, WINDOW, N_KV_HEADS, HEAD_DIM), F32)
    return pl.pallas_call(
        _swa_sample_kernel,
        grid=(nb,),
        in_specs=[pl.BlockSpec(memory_space=pltpu.SMEM), q_spec, page_spec, page_spec, page_spec, page_spec],
        out_specs=[q_spec, page_spec, page_spec],
        out_shape=[jax.ShapeDtypeStruct(q_rows.shape, F32), cache_shape, cache_shape],
        compiler_params=_params("parallel"),
        name="swa_sample",
    )(sinks, q_rows, k_own, v_own, cache_k, cache_v)


def _head_concat(flat_ref, first_token, n_tokens):
    return jnp.concatenate(
        [flat_ref[pl.ds(first_token * N_KV_HEADS + g, n_tokens, stride=N_KV_HEADS), :]
         for g in range(N_KV_HEADS)], axis=1)


def _lane_to_rows(row_vec):
    i = lax.broadcasted_iota(jnp.int32, (LANES, LANES), 0)
    j = lax.broadcasted_iota(jnp.int32, (LANES, LANES), 1)
    return jnp.sum(jnp.where(i == j, row_vec, 0.0), axis=1, keepdims=True)


def _moba_sample_kernel(pt_ref, qbd_ref, kn_ref, vn_ref, kpool, vpool, o_ref,
                        buf, sem, scores, means, sel, acc, *, n_pages):
    b = pl.program_id(0)
    chunk_tokens = PAGES_PER_CHUNK * PAGE_SIZE
    n_chunks = n_pages // PAGES_PER_CHUNK
    blocks_per_chunk = chunk_tokens // MOBA_BLOCK
    n_past = n_pages * PAGE_SIZE // MOBA_BLOCK

    def chunk_copy(pool, ci, p):
        page = pt_ref[b * n_pages + ci * PAGES_PER_CHUNK + p]
        slot = ci % CHUNK_SLOTS
        return pltpu.make_async_copy(pool.at[page], buf.at[slot, pl.ds(p * PAGE_ROWS, PAGE_ROWS)],
                                     sem.at[slot, p])

    def start_chunk(pool, ci):
        for p in range(PAGES_PER_CHUNK):
            chunk_copy(pool, ci, p).start()

    def wait_chunk(pool, ci):
        for p in range(PAGES_PER_CHUNK):
            chunk_copy(pool, ci, p).wait()

    def prime(pool):
        for ci in range(CHUNK_SLOTS - 1):
            start_chunk(pool, ci)

    def stream(pool, per_block):
        def body(ci, carry):
            @pl.when(ci + CHUNK_SLOTS - 1 < n_chunks)
            def _():
                start_chunk(pool, ci + CHUNK_SLOTS - 1)

            wait_chunk(pool, ci)
            flat = buf.at[ci % CHUNK_SLOTS]
            for h in range(blocks_per_chunk):
                per_block(ci * blocks_per_chunk + h, _head_concat(flat, h * MOBA_BLOCK, MOBA_BLOCK))
            return carry

        lax.fori_loop(0, n_chunks, body, 0)

    def scores_of(rows, qbd):
        return jnp.dot(rows.astype(BF16), qbd, preferred_element_type=F32) * SCALE

    def weighted_values(p, rows):
        return lax.dot_general(p.astype(BF16), rows.astype(BF16), TN_DIMS, preferred_element_type=F32)

    qbd_f32 = qbd_ref[0]
    qbd = qbd_f32.astype(BF16)

    def k_block(n, rows):
        means[pl.ds(n, 1), :] = jnp.sum(rows, axis=0, keepdims=True) / MOBA_BLOCK
        off = pl.multiple_of(n * MOBA_BLOCK, MOBA_BLOCK)
        scores[pl.ds(off, MOBA_BLOCK), :] = scores_of(rows, qbd)

    prime(kpool)
    stream(kpool, k_block)
    prime(vpool)

    gate = jnp.dot(means[...], qbd_f32, precision=lax.Precision.HIGHEST, preferred_element_type=F32)
    blk = lax.broadcasted_iota(jnp.int32, gate.shape, 0).astype(F32)
    chosen = jnp.zeros(gate.shape, F32)
    for _ in range(MOBA_TOPK):
        best = jnp.max(gate, axis=0, keepdims=True)
        first = jnp.min(jnp.where(gate == best, blk, float(n_past)), axis=0, keepdims=True)
        hit = blk == first
        chosen = jnp.where(hit, 1.0, chosen)
        gate = jnp.where(hit, NEG_INF, gate)
    sel[...] = chosen

    own_k = _head_concat(kn_ref.at[0], 0, PAGE_SIZE)
    own_v = _head_concat(vn_ref.at[0], 0, PAGE_SIZE)
    s_own = scores_of(own_k, qbd)
    row = lax.broadcasted_iota(jnp.int32, s_own.shape, 0)
    t = lax.broadcasted_iota(jnp.int32, s_own.shape, 1) % DEC_SEQ
    s_own = jnp.where(row <= t, s_own, NEG_INF)

    def block_scores(n):
        off = pl.multiple_of(n * MOBA_BLOCK, MOBA_BLOCK)
        return off, jnp.where(sel[pl.ds(n, 1), :] > 0.0, scores[pl.ds(off, MOBA_BLOCK), :], NEG_INF)

    def max_body(n, m):
        return jnp.maximum(m, jnp.max(block_scores(n)[1], axis=0, keepdims=True))

    m = lax.fori_loop(0, n_past, max_body, jnp.max(s_own, axis=0, keepdims=True))
    p_own = jnp.exp(s_own - m)

    def exp_body(n, l):
        off, s = block_scores(n)
        p = jnp.exp(s - m)
        scores[pl.ds(off, MOBA_BLOCK), :] = p
        return l + jnp.sum(p, axis=0, keepdims=True)

    l = lax.fori_loop(0, n_past, exp_body, jnp.sum(p_own, axis=0, keepdims=True))

    acc[...] = weighted_values(p_own, own_v)

    def v_block(n, rows):
        off = pl.multiple_of(n * MOBA_BLOCK, MOBA_BLOCK)
        acc[...] += weighted_values(scores[pl.ds(off, MOBA_BLOCK), :], rows)

    stream(vpool, v_block)

    inv = 1.0 / _lane_to_rows(l)
    for g in range(N_KV_HEADS):
        rows = slice(g * ROWS_PER_KV, (g + 1) * ROWS_PER_KV)
        o_ref[0, g] = acc[rows, g * HEAD_DIM:(g + 1) * HEAD_DIM] * inv[rows]


def moba_sample(page_table, qbd, k_own, v_own, pool_k, pool_v):
    nb, n_pages = page_table.shape
    n_past = n_pages * PAGE_SIZE // MOBA_BLOCK
    page_spec = pl.BlockSpec((1, PAGE_ROWS, HEAD_DIM), lambda b, pt: (b, 0, 0))
    grid_spec = pltpu.PrefetchScalarGridSpec(
        num_scalar_prefetch=1,
        grid=(nb,),
        in_specs=[pl.BlockSpec((1, KV_DIM, LANES), lambda b, pt: (b, 0, 0)), page_spec, page_spec,
                  pl.BlockSpec(memory_space=pl.ANY), pl.BlockSpec(memory_space=pl.ANY)],
        out_specs=pl.BlockSpec((1, N_KV_HEADS, ROWS_PER_KV, HEAD_DIM), lambda b, pt: (b, 0, 0, 0)),
        scratch_shapes=[
            pltpu.VMEM((CHUNK_SLOTS, PAGES_PER_CHUNK * PAGE_ROWS, HEAD_DIM), F32),
            pltpu.SemaphoreType.DMA((CHUNK_SLOTS, PAGES_PER_CHUNK)),
            pltpu.VMEM((n_past * MOBA_BLOCK, LANES), F32),
            pltpu.VMEM((n_past, KV_DIM), F32),
            pltpu.VMEM((n_past, LANES), F32),
            pltpu.VMEM((LANES, KV_DIM), F32),
        ],
    )
    return pl.pallas_call(
        functools.partial(_moba_sample_kernel, n_pages=n_pages),
        grid_spec=grid_spec,
        out_shape=jax.ShapeDtypeStruct((nb, N_KV_HEADS, ROWS_PER_KV, HEAD_DIM), F32),
        compiler_params=_params("arbitrary"),
        name="moba_sample",
    )(page_table.reshape(-1), qbd, k_own, v_own, pool_k, pool_v)


def sample_query_rows(q):
    nb, t, _ = q.shape
    q = q.reshape(nb, t, N_KV_HEADS, GROUP, HEAD_DIM).transpose(0, 2, 3, 1, 4)
    return q.reshape(nb, N_KV_HEADS, GROUP * t, HEAD_DIM)


def block_diag_queries(q_rows):
    nb = q_rows.shape[0]
    eye = jnp.eye(N_KV_HEADS, dtype=q_rows.dtype)
    return jnp.einsum("bgrd,gh->bgdhr", q_rows, eye).reshape(nb, KV_DIM, N_KV_HEADS * ROWS_PER_KV)


def heads_from_rows(o, t):
    nb = o.shape[0]
    o = o.reshape(nb, N_KV_HEADS, GROUP, t, HEAD_DIM).transpose(0, 3, 1, 2, 4)
    return o.reshape(nb * t, N_HEADS * HEAD_DIM)


def own_page(x):
    nb, t, _ = x.shape
    x = x.reshape(nb, t, N_KV_HEADS, HEAD_DIM)
    return jnp.pad(x, ((0, 0), (0, PAGE_SIZE - t), (0, 0), (0, 0)))


def kernel(x_prompt, x_sample, cache_swa_k, cache_swa_v, cache_moba_k, cache_moba_v, page_table, attn_norm, w_qkv, w_o, swa_sinks, ffn1_norm, ffn1_w_gate, ffn1_w_up, ffn1_w_down, ffn2_norm, ffn2_w_gate, ffn2_w_up, ffn2_w_down, final_norm):
    batch, seq, d = x_prompt.shape
    dec_batch, dec_seq, _ = x_sample.shape
    assert dec_seq == DEC_SEQ
    mp = batch * seq
    depth = w_qkv.shape[0]
    q_dim = N_HEADS * HEAD_DIM

    x = jnp.concatenate([x_prompt.reshape(mp, d), x_sample.reshape(dec_batch * dec_seq, d)], axis=0)
    pos = jnp.concatenate([jnp.tile(jnp.arange(seq), batch), jnp.tile(PAST_LEN + jnp.arange(dec_seq), dec_batch)])
    cosf, sinf = rope_tables(pos)

    ffn = ((ffn1_norm, ffn1_w_gate, ffn1_w_up, ffn1_w_down),
           (ffn2_norm, ffn2_w_gate, ffn2_w_up, ffn2_w_down))

    def half_ffn(x, which, layer, side=()):
        norm, wg, wu, wd = ffn[which]
        a, wd_bf16 = gate_up(rmsnorm(x, norm[layer], BF16), wg, wu, wd, layer)
        return matmul_residual(a, wd_bf16[None], 0, x, DOWN_TN, DOWN_TK, side=side)

    outs = {}
    for i in range(depth):
        a_idx = i // 2
        x, w_qkv_bf16, w_o_bf16 = half_ffn(x, 0, i, side=((w_qkv, i), (w_o, i)))
        qkv = qkv_rope(rmsnorm(x, attn_norm[i], BF16), w_qkv_bf16[None], 0, cosf, sinf)
        qkv_s = qkv[mp:].reshape(dec_batch, dec_seq, -1)
        k_new_p = qkv[:mp, q_dim:q_dim + KV_DIM].reshape(batch, seq, N_KV_HEADS, HEAD_DIM)
        v_new_p = qkv[:mp, q_dim + KV_DIM:].reshape(batch, seq, N_KV_HEADS, HEAD_DIM)
        k_new_s = qkv_s[..., q_dim:q_dim + KV_DIM]
        v_new_s = qkv_s[..., q_dim + KV_DIM:]
        q_rows = sample_query_rows(qkv_s[..., :q_dim])
        if i % 2 == 0:
            o_p = swa_prompt(qkv, swa_sinks[a_idx], batch, seq)
            o_s, nk, nv = swa_sample(swa_sinks[a_idx], q_rows, own_page(k_new_s), own_page(v_new_s),
                                     cache_swa_k[a_idx], cache_swa_v[a_idx])
            outs.setdefault("swa_kp", []).append(k_new_p[:, -WINDOW:])
            outs.setdefault("swa_vp", []).append(v_new_p[:, -WINDOW:])
            outs.setdefault("swa_ks", []).append(nk)
            outs.setdefault("swa_vs", []).append(nv)
        else:
            o_p = moba_prompt(qkv, batch, seq)
            flat = (-1, PAGE_ROWS, HEAD_DIM)
            o_s = moba_sample(page_table, block_diag_queries(q_rows),
                              own_page(k_new_s).reshape(flat), own_page(v_new_s).reshape(flat),
                              cache_moba_k[a_idx].reshape(flat), cache_moba_v[a_idx].reshape(flat))
            outs.setdefault("moba_kp", []).append(k_new_p)
            outs.setdefault("moba_vp", []).append(v_new_p)
            outs.setdefault("moba_ks", []).append(k_new_s.reshape(dec_batch, dec_seq, N_KV_HEADS, HEAD_DIM))
            outs.setdefault("moba_vs", []).append(v_new_s.reshape(dec_batch, dec_seq, N_KV_HEADS, HEAD_DIM))
        o = jnp.concatenate([o_p, heads_from_rows(o_s, dec_seq).astype(BF16)], axis=0)
        x = matmul_residual(o, w_o_bf16[None], 0, x, OUT_TN, OUT_TK)
        x = half_ffn(x, 1, i)

    ms = dec_batch * dec_seq
    y_prompt = rmsnorm(x, final_norm, F32, tile=FINAL_TILE, rows=mp).reshape(batch, seq, d)
    y_sample = rmsnorm(x, final_norm, F32, tile=ms, first_tile=mp // ms, rows=ms).reshape(dec_batch, dec_seq, d)
    return (y_prompt, y_sample,
            jnp.stack(outs["swa_kp"]), jnp.stack(outs["swa_vp"]),
            jnp.stack(outs["swa_ks"]), jnp.stack(outs["swa_vs"]),
            jnp.stack(outs["moba_kp"]), jnp.stack(outs["moba_vp"]),
            jnp.stack(outs["moba_ks"]), jnp.stack(outs["moba_vs"]))
```

```python
import functools

import jax
import jax.numpy as jnp
from jax import lax
from jax.experimental import pallas as pl
from jax.experimental.pallas import tpu as pltpu

F32 = jnp.float32
BF16 = jnp.bfloat16

D_MODEL = 4096
N_HEADS = 32
N_KV_HEADS = 8
HEAD_DIM = 128
GROUP = N_HEADS // N_KV_HEADS
KV_DIM = N_KV_HEADS * HEAD_DIM
D_FF = 11008
WINDOW = 128
MOBA_BLOCK = 256
MOBA_TOPK = 3
PAGE_SIZE = 128
PAGE_ROWS = PAGE_SIZE * N_KV_HEADS
PAST_LEN = 16384
DEC_SEQ = 4
ROPE_THETA = 10000.0
RMS_EPS = 1e-6
SCALE = HEAD_DIM ** -0.5
LOG2E = 1.4426950408889634

V7X_VMEM_BYTES = 64 * 1024 * 1024
VMEM_LIMIT = V7X_VMEM_BYTES - 8 * 1024 * 1024
LANES = 128

ROW_TILE = 1040
NORM_TILE = 416
FINAL_TILE = 512
SWA_QT = 256
PROMPT_KV_HEADS = 4
GATE_ROW_TILE = 2080
FF_TILE = 256
D_FF_PAD = 11264
DOWN_TN = 1024
DOWN_TK = 2816
QKV_TN = 1024
OUT_TN = 1024
OUT_TK = 2048
PAGES_PER_CHUNK = 4
CHUNK_SLOTS = 8
ROWS_PER_KV = GROUP * DEC_SEQ
NEG_INF = float("-inf")
NT_DIMS = (((1,), (1,)), ((), ()))
TN_DIMS = (((0,), (0,)), ((), ()))


def _params(*sem):
    return pltpu.CompilerParams(dimension_semantics=sem, vmem_limit_bytes=VMEM_LIMIT)


def _rms_kernel(x_ref, g_ref, o_ref):
    x = x_ref[...]
    ms = jnp.mean(x * x, axis=-1, keepdims=True)
    o_ref[...] = ((x * lax.rsqrt(ms + RMS_EPS)) * g_ref[...]).astype(o_ref.dtype)


def rmsnorm(x, gain, out_dtype, tile=NORM_TILE, first_tile=0, rows=None):
    d = x.shape[1]
    rows = x.shape[0] if rows is None else rows
    return pl.pallas_call(
        _rms_kernel,
        grid=(rows // tile,),
        in_specs=[pl.BlockSpec((tile, d), lambda i: (first_tile + i, 0)),
                  pl.BlockSpec((1, d), lambda i: (0, 0))],
        out_specs=pl.BlockSpec((tile, d), lambda i: (i, 0)),
        out_shape=jax.ShapeDtypeStruct((rows, d), out_dtype),
        compiler_params=_params("parallel"),
        name="rmsnorm",
    )(x, gain.reshape(1, d))


def _gateup_kernel(h_ref, wg_ref, wu_ref, wd_ref, o_ref, wdb_ref, *, real_tiles):
    i = pl.program_id(0)
    j = pl.program_id(1)

    @pl.when(j < real_tiles)
    def _():
        h = h_ref[...]
        g = jnp.dot(h, wg_ref[...].astype(BF16), preferred_element_type=F32)
        u = jnp.dot(h, wu_ref[...].astype(BF16), preferred_element_type=F32)
        o_ref[...] = (0.5 * g * jax.nn.sigmoid(g) * u).astype(o_ref.dtype)

    @pl.when(j >= real_tiles)
    def _():
        o_ref[...] = jnp.zeros_like(o_ref)

    @pl.when((i == 0) & (j < real_tiles))
    def _():
        wdb_ref[...] = wd_ref[...].astype(BF16)

    @pl.when((i == 0) & (j >= real_tiles))
    def _():
        wdb_ref[...] = jnp.zeros_like(wdb_ref)


def gate_up(h, wg, wu, wd, layer):
    m, d = h.shape
    real_tiles = wg.shape[2] // FF_TILE
    pad_tiles = D_FF_PAD // FF_TILE
    assert pad_tiles == real_tiles + 1

    def real(j):
        return jnp.minimum(j, real_tiles - 1)

    w_spec = pl.BlockSpec((None, d, FF_TILE), lambda i, j: (layer, 0, real(j)))
    wd_spec = pl.BlockSpec((None, FF_TILE, d), lambda i, j: (layer, jnp.where(i == 0, real(j), real_tiles - 1), 0))
    wdb_spec = pl.BlockSpec((FF_TILE, d), lambda i, j: (jnp.where(i == 0, j, real_tiles), 0))
    return pl.pallas_call(
        functools.partial(_gateup_kernel, real_tiles=real_tiles),
        grid=(m // GATE_ROW_TILE, pad_tiles),
        in_specs=[pl.BlockSpec((GATE_ROW_TILE, d), lambda i, j: (i, 0), pipeline_mode=pl.Buffered(1)),
                  w_spec, w_spec, wd_spec],
        out_specs=[pl.BlockSpec((GATE_ROW_TILE, FF_TILE), lambda i, j: (i, j)), wdb_spec],
        out_shape=[jax.ShapeDtypeStruct((m, D_FF_PAD), BF16), jax.ShapeDtypeStruct((D_FF_PAD, d), BF16)],
        compiler_params=_params("arbitrary", "arbitrary"),
        name="gate_up",
    )(h, wg, wu, wd)


def _matmul_residual_kernel(a_ref, w_ref, r_ref, *rest, n_side):
    side_in, o_ref, side_out = rest[:n_side], rest[n_side], rest[n_side + 1:]
    k = pl.program_id(2)

    @pl.when(k == 0)
    def _():
        o_ref[...] = r_ref[...] + jnp.dot(a_ref[...], w_ref[...].astype(BF16), preferred_element_type=F32)

    @pl.when(k > 0)
    def _():
        o_ref[...] += jnp.dot(a_ref[...], w_ref[...].astype(BF16), preferred_element_type=F32)

    for src, dst in zip(side_in, side_out):
        dst[...] = src[...].astype(BF16)


def matmul_residual(a, w, layer, res, tn, tk, side=()):
    m, kdim = a.shape
    n = w.shape[2]
    gi, gj, gk = m // ROW_TILE, n // tn, kdim // tk
    steps = gi * gj * gk

    def step(i, j, k):
        return (i * gj + j) * gk + k

    side_in, side_out, side_shapes = [], [], []
    for sw, sl in side:
        rows, cols = sw.shape[1] // steps, sw.shape[2]
        assert rows * steps == sw.shape[1]
        side_in.append(pl.BlockSpec((None, rows, cols), lambda i, j, k, sl=sl: (sl, step(i, j, k), 0)))
        side_out.append(pl.BlockSpec((rows, cols), lambda i, j, k: (step(i, j, k), 0)))
        side_shapes.append(jax.ShapeDtypeStruct(sw.shape[1:], BF16))
    out = pl.pallas_call(
        functools.partial(_matmul_residual_kernel, n_side=len(side)),
        grid=(gi, gj, gk),
        in_specs=[pl.BlockSpec((ROW_TILE, tk), lambda i, j, k: (i, k)),
                  pl.BlockSpec((None, tk, tn), lambda i, j, k: (layer, k, j)),
                  pl.BlockSpec((ROW_TILE, tn), lambda i, j, k: (i, j))] + side_in,
        out_specs=[pl.BlockSpec((ROW_TILE, tn), lambda i, j, k: (i, j))] + side_out,
        out_shape=[jax.ShapeDtypeStruct((m, n), F32)] + side_shapes,
        compiler_params=_params("arbitrary", "arbitrary", "arbitrary"),
        name="matmul_residual",
    )(a, w, res, *[sw for sw, _ in side])
    return (out[0], *out[1:]) if side else out[0]


def _qkv_kernel(h_ref, w_ref, cos_ref, sin_ref, o_ref, *, rope_tiles):
    acc = jnp.dot(h_ref[...], w_ref[...].astype(BF16), preferred_element_type=F32)
    n = pl.program_id(1)

    @pl.when(n < rope_tiles)
    def _():
        cosf = cos_ref[...]
        sinf = sin_ref[...]
        for c in range(acc.shape[1] // HEAD_DIM):
            x = acc[:, c * HEAD_DIM:(c + 1) * HEAD_DIM]
            o_ref[:, c * HEAD_DIM:(c + 1) * HEAD_DIM] = x * cosf + pltpu.roll(x, HEAD_DIM // 2, 1) * sinf

    @pl.when(n >= rope_tiles)
    def _():
        o_ref[...] = acc


def qkv_rope(h, w, layer, cosf, sinf):
    m, d = h.shape
    n = w.shape[2]
    rope_tiles = (N_HEADS + N_KV_HEADS) * HEAD_DIM // QKV_TN
    return pl.pallas_call(
        functools.partial(_qkv_kernel, rope_tiles=rope_tiles),
        grid=(m // ROW_TILE, n // QKV_TN),
        in_specs=[pl.BlockSpec((ROW_TILE, d), lambda i, j: (i, 0)),
                  pl.BlockSpec((None, d, QKV_TN), lambda i, j: (layer, 0, j)),
                  pl.BlockSpec((ROW_TILE, HEAD_DIM), lambda i, j: (i, 0)),
                  pl.BlockSpec((ROW_TILE, HEAD_DIM), lambda i, j: (i, 0))],
        out_specs=pl.BlockSpec((ROW_TILE, QKV_TN), lambda i, j: (i, j)),
        out_shape=jax.ShapeDtypeStruct((m, n), F32),
        compiler_params=_params("parallel", "arbitrary"),
        name="qkv_rope",
    )(h, w, cosf, sinf)


def rope_tables(pos):
    inv = ROPE_THETA ** (-jnp.arange(0, HEAD_DIM, 2, dtype=F32) / HEAD_DIM)
    ang = pos.astype(F32)[:, None] * inv[None, :]
    cos = jnp.cos(ang)
    sin = jnp.sin(ang)
    return jnp.concatenate([cos, cos], axis=-1), jnp.concatenate([-sin, sin], axis=-1)


def _swa_prompt_kernel(sink_ref, q_ref, kp_ref, kc_ref, vp_ref, vc_ref, o_ref):
    g = pl.program_id(1)
    n = pl.program_id(2)
    qw = GROUP * HEAD_DIM
    nq = GROUP * SWA_QT
    nk = WINDOW + SWA_QT
    key = lax.broadcasted_iota(jnp.int32, (nk, nq), 0)
    query = lax.broadcasted_iota(jnp.int32, (nk, nq), 1) % SWA_QT
    mask = (key > query) & (key <= query + WINDOW) & ((key >= WINDOW) | (n > 0))
    head = lax.broadcasted_iota(jnp.int32, (1, nq), 1) // SWA_QT
    for h in range(PROMPT_KV_HEADS):
        cols = slice(h * HEAD_DIM, (h + 1) * HEAD_DIM)
        qf = jnp.concatenate([q_ref[:, h * qw + j * HEAD_DIM:h * qw + (j + 1) * HEAD_DIM] for j in range(GROUP)], axis=0)
        q = (qf * (SCALE * LOG2E)).astype(BF16)
        k = jnp.concatenate([kp_ref[:, cols], kc_ref[:, cols]], axis=0).astype(BF16)
        v = jnp.concatenate([vp_ref[:, cols], vc_ref[:, cols]], axis=0).astype(BF16)
        s = jnp.where(mask, lax.dot_general(k, q, NT_DIMS, preferred_element_type=F32), NEG_INF)
        sink = jnp.zeros((1, nq), F32)
        for j in range(GROUP):
            sink = jnp.where(head == j, sink_ref[(g * PROMPT_KV_HEADS + h) * GROUP + j] * LOG2E, sink)
        m = jnp.maximum(jnp.max(s, axis=0, keepdims=True), sink)
        p = jnp.exp2(s - m)
        denom = jnp.sum(p, axis=0, keepdims=True) + jnp.exp2(sink - m)
        out = lax.dot_general(v, p.astype(BF16), TN_DIMS, preferred_element_type=F32) / denom
        for j in range(GROUP):
            o_ref[:, h * qw + j * HEAD_DIM:h * qw + (j + 1) * HEAD_DIM] = out[:, j * SWA_QT:(j + 1) * SWA_QT].T.astype(o_ref.dtype)


def swa_prompt(qkv, sinks, batch, seq):
    nt = seq // SWA_QT
    wpt = SWA_QT // WINDOW
    kw = PROMPT_KV_HEADS * HEAD_DIM
    kcol = N_HEADS * HEAD_DIM // kw
    vcol = (N_HEADS + N_KV_HEADS) * HEAD_DIM // kw
    qw = PROMPT_KV_HEADS * GROUP * HEAD_DIM

    def prev_window(col):
        return pl.BlockSpec((WINDOW, kw), lambda b, g, n: ((b * nt + n) * wpt - jnp.minimum(n, 1), col + g))

    def tile(col):
        return pl.BlockSpec((SWA_QT, kw), lambda b, g, n: (b * nt + n, col + g))

    return pl.pallas_call(
        _swa_prompt_kernel,
        grid=(batch, N_KV_HEADS // PROMPT_KV_HEADS, nt),
        in_specs=[
            pl.BlockSpec(memory_space=pltpu.SMEM),
            pl.BlockSpec((SWA_QT, qw), lambda b, g, n: (b * nt + n, g)),
            prev_window(kcol), tile(kcol), prev_window(vcol), tile(vcol),
        ],
        out_specs=pl.BlockSpec((SWA_QT, qw), lambda b, g, n: (b * nt + n, g)),
        out_shape=jax.ShapeDtypeStruct((batch * seq, N_HEADS * HEAD_DIM), BF16),
        compiler_params=_params("parallel", "parallel", "arbitrary"),
        name="swa_prompt",
    )(sinks, qkv, qkv, qkv, qkv, qkv)


def _moba_prompt_kernel(q_ref, k_ref, v_ref, o_ref, means_ref, sel_ref, acc_ref, *, nblk):
    c = pl.program_id(2)
    qw = GROUP * HEAD_DIM

    @pl.when(c == 0)
    def _():
        for n in range(nblk):
            kb = k_ref[n * MOBA_BLOCK:(n + 1) * MOBA_BLOCK, :]
            means_ref[n:n + 1, :] = jnp.sum(kb, axis=0, keepdims=True) / MOBA_BLOCK

    qs = []
    for h in range(PROMPT_KV_HEADS):
        qf = jnp.concatenate([q_ref[:, h * qw + j * HEAD_DIM:h * qw + (j + 1) * HEAD_DIM] for j in range(GROUP)], axis=0)
        qs.append((qf * (SCALE * LOG2E)).astype(BF16))
        gate = lax.dot_general(means_ref[:, h * HEAD_DIM:(h + 1) * HEAD_DIM], qf, NT_DIMS,
                               precision=lax.Precision.HIGHEST, preferred_element_type=F32)
        blk = lax.broadcasted_iota(jnp.int32, gate.shape, 0)
        gate = jnp.where(blk < c, gate, NEG_INF)
        for n in range(nblk):
            gate_n = gate[n:n + 1, :]
            ahead = (gate > gate_n) | ((gate == gate_n) & (blk < n))
            rank = jnp.sum(jnp.where(ahead, 1.0, 0.0), axis=0, keepdims=True)
            sel_ref[h, n:n + 1, :] = jnp.where(rank < MOBA_TOPK, 1.0, 0.0)

    def block_scores(h, off):
        kb = k_ref[pl.ds(off, MOBA_BLOCK), h * HEAD_DIM:(h + 1) * HEAD_DIM].astype(BF16)
        return lax.dot_general(kb, qs[h], NT_DIMS, preferred_element_type=F32)

    def weighted_values(h, off, p):
        vb = v_ref[pl.ds(off, MOBA_BLOCK), h * HEAD_DIM:(h + 1) * HEAD_DIM].astype(BF16)
        return lax.dot_general(vb, p.astype(BF16), TN_DIMS, preferred_element_type=F32)

    own = pl.multiple_of(c * MOBA_BLOCK, MOBA_BLOCK)
    init = []
    for h in range(PROMPT_KV_HEADS):
        s = block_scores(h, own)
        key = lax.broadcasted_iota(jnp.int32, s.shape, 0)
        query = lax.broadcasted_iota(jnp.int32, s.shape, 1) % MOBA_BLOCK
        s = jnp.where(key <= query, s, NEG_INF)
        m0 = jnp.max(s, axis=0, keepdims=True)
        p = jnp.exp2(s - m0)
        init.append((m0, jnp.sum(p, axis=0, keepdims=True)))
        acc_ref[h] = weighted_values(h, own, p)

    def past_block(n, carry):
        off = pl.multiple_of(n * MOBA_BLOCK, MOBA_BLOCK)
        out = []
        for h in range(PROMPT_KV_HEADS):
            m, l = carry[h]
            sb = jnp.where(sel_ref[h, pl.ds(n, 1), :] > 0.0, block_scores(h, off), NEG_INF)
            m_new = jnp.maximum(m, jnp.max(sb, axis=0, keepdims=True))
            alpha = jnp.exp2(m - m_new)
            pb = jnp.exp2(sb - m_new)
            acc_ref[h] = alpha * acc_ref[h] + weighted_values(h, off, pb)
            out.append((m_new, alpha * l + jnp.sum(pb, axis=0, keepdims=True)))
        return tuple(out)

    final = lax.fori_loop(0, c, past_block, tuple(init))
    for h in range(PROMPT_KV_HEADS):
        out = acc_ref[h] / final[h][1]
        for j in range(GROUP):
            o_ref[:, h * qw + j * HEAD_DIM:h * qw + (j + 1) * HEAD_DIM] = out[:, j * MOBA_BLOCK:(j + 1) * MOBA_BLOCK].T.astype(o_ref.dtype)


def moba_prompt(qkv, batch, seq):
    nblk = seq // MOBA_BLOCK
    kw = PROMPT_KV_HEADS * HEAD_DIM
    kcol = N_HEADS * HEAD_DIM // kw
    vcol = (N_HEADS + N_KV_HEADS) * HEAD_DIM // kw
    qw = PROMPT_KV_HEADS * GROUP * HEAD_DIM
    nq = GROUP * MOBA_BLOCK
    return pl.pallas_call(
        functools.partial(_moba_prompt_kernel, nblk=nblk),
        grid=(batch, N_KV_HEADS // PROMPT_KV_HEADS, nblk),
        in_specs=[
            pl.BlockSpec((MOBA_BLOCK, qw), lambda b, g, c: (b * nblk + c, g)),
            pl.BlockSpec((seq, kw), lambda b, g, c: (b, kcol + g)),
            pl.BlockSpec((seq, kw), lambda b, g, c: (b, vcol + g)),
        ],
        out_specs=pl.BlockSpec((MOBA_BLOCK, qw), lambda b, g, c: (b * nblk + c, g)),
        out_shape=jax.ShapeDtypeStruct((batch * seq, N_HEADS * HEAD_DIM), BF16),
        scratch_shapes=[pltpu.VMEM((nblk, kw), F32),
                        pltpu.VMEM((PROMPT_KV_HEADS, nblk, nq), F32),
                        pltpu.VMEM((PROMPT_KV_HEADS, HEAD_DIM, nq), F32)],
        compiler_params=_params("parallel", "parallel", "arbitrary"),
        name="moba_prompt",
    )(qkv, qkv, qkv)


def _swa_sample_kernel(sink_ref, q_ref, kn_ref, vn_ref, ck_ref, cv_ref, o_ref, nk_ref, nv_ref):
    keep = WINDOW - DEC_SEQ
    nk_ref[0, 0:keep] = ck_ref[0, DEC_SEQ:WINDOW]
    nk_ref[0, keep:WINDOW] = kn_ref[0, 0:DEC_SEQ]
    nv_ref[0, 0:keep] = cv_ref[0, DEC_SEQ:WINDOW]
    nv_ref[0, keep:WINDOW] = vn_ref[0, 0:DEC_SEQ]

    t = lax.broadcasted_iota(jnp.int32, (ROWS_PER_KV, 2 * WINDOW), 0) % DEC_SEQ
    col = lax.broadcasted_iota(jnp.int32, (ROWS_PER_KV, 2 * WINDOW), 1)
    mask = ((col < WINDOW) & (col > t)) | ((col >= WINDOW) & (col - WINDOW <= t))
    row = lax.broadcasted_iota(jnp.int32, (ROWS_PER_KV, 1), 0)
    for g in range(N_KV_HEADS):
        q = q_ref[0, g].astype(BF16)
        k = jnp.concatenate([ck_ref[0, :, g, :], kn_ref[0, :, g, :]], axis=0).astype(BF16)
        v = jnp.concatenate([cv_ref[0, :, g, :], vn_ref[0, :, g, :]], axis=0).astype(BF16)
        s = lax.dot_general(q, k, NT_DIMS, preferred_element_type=F32) * SCALE
        s = jnp.where(mask, s, NEG_INF)
        sink = jnp.zeros((ROWS_PER_KV, 1), F32)
        for j in range(GROUP):
            sink = jnp.where(row // DEC_SEQ == j, sink_ref[g * GROUP + j], sink)
        m = jnp.maximum(jnp.max(s, axis=-1, keepdims=True), sink)
        p = jnp.exp(s - m)
        denom = jnp.sum(p, axis=-1, keepdims=True) + jnp.exp(sink - m)
        o_ref[0, g] = jnp.dot(p.astype(BF16), v, preferred_element_type=F32) / denom


def swa_sample(sinks, q_rows, k_own, v_own, cache_k, cache_v):
    nb = q_rows.shape[0]
    page_spec = pl.BlockSpec((1, PAGE_SIZE, N_KV_HEADS, HEAD_DIM), lambda b: (b, 0, 0, 0))
    q_spec = pl.BlockSpec((1, N_KV_HEADS, ROWS_PER_KV, HEAD_DIM), lambda b: (b, 0, 0, 0))
    cache_shape = jax.ShapeDtypeStruct((nb, WINDOW, N_KV_HEADS, HEAD_DIM), F32)
    return pl.pallas_call(
        _swa_sample_kernel,
        grid=(nb,),
        in_specs=[pl.BlockSpec(memory_space=pltpu.SMEM), q_spec, page_spec, page_spec, page_spec, page_spec],
        out_specs=[q_spec, page_spec, page_spec],
        out_shape=[jax.ShapeDtypeStruct(q_rows.shape, F32), cache_shape, cache_shape],
        compiler_params=_params("parallel"),
        name="swa_sample",
    )(sinks, q_rows, k_own, v_own, cache_k, cache_v)


def _head_concat(flat_ref, first_token, n_tokens):
    return jnp.concatenate(
        [flat_ref[pl.ds(first_token * N_KV_HEADS + g, n_tokens, stride=N_KV_HEADS), :]
         for g in range(N_KV_HEADS)], axis=1)


def _lane_to_rows(row_vec):
    i = lax.broadcasted_iota(jnp.int32, (LANES, LANES), 0)
    j = lax.broadcasted_iota(jnp.int32, (LANES, LANES), 1)
    return jnp.sum(jnp.where(i == j, row_vec, 0.0), axis=1, keepdims=True)


def _moba_sample_kernel(pt_ref, qbd_ref, kn_ref, vn_ref, kpool, vpool, o_ref,
                        buf, sem, scores, means, sel, acc, *, n_pages):
    b = pl.program_id(0)
    chunk_tokens = PAGES_PER_CHUNK * PAGE_SIZE
    n_chunks = n_pages // PAGES_PER_CHUNK
    blocks_per_chunk = chunk_tokens // MOBA_BLOCK
    n_past = n_pages * PAGE_SIZE // MOBA_BLOCK

    assert n_chunks % CHUNK_SLOTS == 0
    lookahead = CHUNK_SLOTS - 1

    def chunk_copy(pool, sample, ci, p):
        page = pt_ref[sample * n_pages + ci * PAGES_PER_CHUNK + p]
        slot = ci % CHUNK_SLOTS
        return pltpu.make_async_copy(pool.at[page], buf.at[slot, pl.ds(p * PAGE_ROWS, PAGE_ROWS)],
                                     sem.at[slot, p])

    def start_chunk(pool, sample, ci):
        for p in range(PAGES_PER_CHUNK):
            chunk_copy(pool, sample, ci, p).start()

    def wait_chunk(pool, ci):
        for p in range(PAGES_PER_CHUNK):
            chunk_copy(pool, b, ci, p).wait()

    def stream(pool, per_block, next_pool, next_sample, has_next):
        def body(ci, carry):
            @pl.when(ci + lookahead < n_chunks)
            def _():
                start_chunk(pool, b, ci + lookahead)

            @pl.when((ci + lookahead >= n_chunks) & has_next)
            def _():
                start_chunk(next_pool, next_sample, ci + lookahead - n_chunks)

            wait_chunk(pool, ci)
            flat = buf.at[ci % CHUNK_SLOTS]
            for h in range(blocks_per_chunk):
                per_block(ci * blocks_per_chunk + h, _head_concat(flat, h * MOBA_BLOCK, MOBA_BLOCK))
            return carry

        lax.fori_loop(0, n_chunks, body, 0)

    def scores_of(rows, qbd):
        return jnp.dot(rows.astype(BF16), qbd, preferred_element_type=F32) * SCALE

    def weighted_values(p, rows):
        return lax.dot_general(p.astype(BF16), rows.astype(BF16), TN_DIMS, preferred_element_type=F32)

    qbd_f32 = qbd_ref[0]
    qbd = qbd_f32.astype(BF16)

    def k_block(n, rows):
        means[pl.ds(n, 1), :] = jnp.sum(rows, axis=0, keepdims=True) / MOBA_BLOCK
        off = pl.multiple_of(n * MOBA_BLOCK, MOBA_BLOCK)
        scores[pl.ds(off, MOBA_BLOCK), :] = scores_of(rows, qbd)

    @pl.when(b == 0)
    def _():
        for ci in range(lookahead):
            start_chunk(kpool, b, ci)

    stream(kpool, k_block, vpool, b, True)

    gate = jnp.dot(means[...], qbd_f32, precision=lax.Precision.HIGHEST, preferred_element_type=F32)
    blk = lax.broadcasted_iota(jnp.int32, gate.shape, 0).astype(F32)
    chosen = jnp.zeros(gate.shape, F32)
    for _ in range(MOBA_TOPK):
        best = jnp.max(gate, axis=0, keepdims=True)
        first = jnp.min(jnp.where(gate == best, blk, float(n_past)), axis=0, keepdims=True)
        hit = blk == first
        chosen = jnp.where(hit, 1.0, chosen)
        gate = jnp.where(hit, NEG_INF, gate)
    sel[...] = chosen

    own_k = _head_concat(kn_ref.at[0], 0, PAGE_SIZE)
    own_v = _head_concat(vn_ref.at[0], 0, PAGE_SIZE)
    s_own = scores_of(own_k, qbd)
    row = lax.broadcasted_iota(jnp.int32, s_own.shape, 0)
    t = lax.broadcasted_iota(jnp.int32, s_own.shape, 1) % DEC_SEQ
    s_own = jnp.where(row <= t, s_own, NEG_INF)

    def block_scores(n):
        off = pl.multiple_of(n * MOBA_BLOCK, MOBA_BLOCK)
        return off, jnp.where(sel[pl.ds(n, 1), :] > 0.0, scores[pl.ds(off, MOBA_BLOCK), :], NEG_INF)

    def max_body(n, m):
        return jnp.maximum(m, jnp.max(block_scores(n)[1], axis=0, keepdims=True))

    m = lax.fori_loop(0, n_past, max_body, jnp.max(s_own, axis=0, keepdims=True))
    p_own = jnp.exp(s_own - m)

    def exp_body(n, l):
        off, s = block_scores(n)
        p = jnp.exp(s - m)
        scores[pl.ds(off, MOBA_BLOCK), :] = p
        return l + jnp.sum(p, axis=0, keepdims=True)

    l = lax.fori_loop(0, n_past, exp_body, jnp.sum(p_own, axis=0, keepdims=True))

    acc[...] = weighted_values(p_own, own_v)

    def v_block(n, rows):
        off = pl.multiple_of(n * MOBA_BLOCK, MOBA_BLOCK)
        acc[...] += weighted_values(scores[pl.ds(off, MOBA_BLOCK), :], rows)

    stream(vpool, v_block, kpool, b + 1, b + 1 < pl.num_programs(0))

    inv = 1.0 / _lane_to_rows(l)
    for g in range(N_KV_HEADS):
        rows = slice(g * ROWS_PER_KV, (g + 1) * ROWS_PER_KV)
        o_ref[0, g] = acc[rows, g * HEAD_DIM:(g + 1) * HEAD_DIM] * inv[rows]


def moba_sample(page_table, qbd, k_own, v_own, pool_k, pool_v):
    nb, n_pages = page_table.shape
    n_past = n_pages * PAGE_SIZE // MOBA_BLOCK
    page_spec = pl.BlockSpec((1, PAGE_ROWS, HEAD_DIM), lambda b, pt: (b, 0, 0))
    grid_spec = pltpu.PrefetchScalarGridSpec(
        num_scalar_prefetch=1,
        grid=(nb,),
        in_specs=[pl.BlockSpec((1, KV_DIM, LANES), lambda b, pt: (b, 0, 0)), page_spec, page_spec,
                  pl.BlockSpec(memory_space=pl.ANY), pl.BlockSpec(memory_space=pl.ANY)],
        out_specs=pl.BlockSpec((1, N_KV_HEADS, ROWS_PER_KV, HEAD_DIM), lambda b, pt: (b, 0, 0, 0)),
        scratch_shapes=[
            pltpu.VMEM((CHUNK_SLOTS, PAGES_PER_CHUNK * PAGE_ROWS, HEAD_DIM), F32),
            pltpu.SemaphoreType.DMA((CHUNK_SLOTS, PAGES_PER_CHUNK)),
            pltpu.VMEM((n_past * MOBA_BLOCK, LANES), F32),
            pltpu.VMEM((n_past, KV_DIM), F32),
            pltpu.VMEM((n_past, LANES), F32),
            pltpu.VMEM((LANES, KV_DIM), F32),
        ],
    )
    return pl.pallas_call(
        functools.partial(_moba_sample_kernel, n_pages=n_pages),
        grid_spec=grid_spec,
        out_shape=jax.ShapeDtypeStruct((nb, N_KV_HEADS, ROWS_PER_KV, HEAD_DIM), F32),
        compiler_params=_params("arbitrary"),
        name="moba_sample",
    )(page_table.reshape(-1), qbd, k_own, v_own, pool_k, pool_v)


def sample_query_rows(q):
    nb, t, _ = q.shape
    q = q.reshape(nb, t, N_KV_HEADS, GROUP, HEAD_DIM).transpose(0, 2, 3, 1, 4)
    return q.reshape(nb, N_KV_HEADS, GROUP * t, HEAD_DIM)


def block_diag_queries(q_rows):
    nb = q_rows.shape[0]
    eye = jnp.eye(N_KV_HEADS, dtype=q_rows.dtype)
    return jnp.einsum("bgrd,gh->bgdhr", q_rows, eye).reshape(nb, KV_DIM, N_KV_HEADS * ROWS_PER_KV)


def heads_from_rows(o, t):
    nb = o.shape[0]
    o = o.reshape(nb, N_KV_HEADS, GROUP, t, HEAD_DIM).transpose(0, 3, 1, 2, 4)
    return o.reshape(nb * t, N_HEADS * HEAD_DIM)


def own_page(x):
    nb, t, _ = x.shape
    x = x.reshape(nb, t, N_KV_HEADS, HEAD_DIM)
    return jnp.pad(x, ((0, 0), (0, PAGE_SIZE - t), (0, 0), (0, 0)))


def kernel(x_prompt, x_sample, cache_swa_k, cache_swa_v, cache_moba_k, cache_moba_v, page_table, attn_norm, w_qkv, w_o, swa_sinks, ffn1_norm, ffn1_w_gate, ffn1_w_up, ffn1_w_down, ffn2_norm, ffn2_w_gate, ffn2_w_up, ffn2_w_down, final_norm):
    batch, seq, d = x_prompt.shape
    dec_batch, dec_seq, _ = x_sample.shape
    assert dec_seq == DEC_SEQ
    mp = batch * seq
    depth = w_qkv.shape[0]
    q_dim = N_HEADS * HEAD_DIM

    x = jnp.concatenate([x_prompt.reshape(mp, d), x_sample.reshape(dec_batch * dec_seq, d)], axis=0)
    pos = jnp.concatenate([jnp.tile(jnp.arange(seq), batch), jnp.tile(PAST_LEN + jnp.arange(dec_seq), dec_batch)])
    cosf, sinf = rope_tables(pos)

    ffn = ((ffn1_norm, ffn1_w_gate, ffn1_w_up, ffn1_w_down),
           (ffn2_norm, ffn2_w_gate, ffn2_w_up, ffn2_w_down))

    def half_ffn(x, which, layer, side=()):
        norm, wg, wu, wd = ffn[which]
        a, wd_bf16 = gate_up(rmsnorm(x, norm[layer], BF16), wg, wu, wd, layer)
        return matmul_residual(a, wd_bf16[None], 0, x, DOWN_TN, DOWN_TK, side=side)

    outs = {}
    for i in range(depth):
        a_idx = i // 2
        x, w_qkv_bf16, w_o_bf16 = half_ffn(x, 0, i, side=((w_qkv, i), (w_o, i)))
        qkv = qkv_rope(rmsnorm(x, attn_norm[i], BF16), w_qkv_bf16[None], 0, cosf, sinf)
        qkv_s = qkv[mp:].reshape(dec_batch, dec_seq, -1)
        k_new_p = qkv[:mp, q_dim:q_dim + KV_DIM].reshape(batch, seq, N_KV_HEADS, HEAD_DIM)
        v_new_p = qkv[:mp, q_dim + KV_DIM:].reshape(batch, seq, N_KV_HEADS, HEAD_DIM)
        k_new_s = qkv_s[..., q_dim:q_dim + KV_DIM]
        v_new_s = qkv_s[..., q_dim + KV_DIM:]
        q_rows = sample_query_rows(qkv_s[..., :q_dim])
        if i % 2 == 0:
            o_p = swa_prompt(qkv, swa_sinks[a_idx], batch, seq)
            o_s, nk, nv = swa_sample(swa_sinks[a_idx], q_rows, own_page(k_new_s), own_page(v_new_s),
                                     cache_swa_k[a_idx], cache_swa_v[a_idx])
            outs.setdefault("swa_kp", []).append(k_new_p[:, -WINDOW:])
            outs.setdefault("swa_vp", []).append(v_new_p[:, -WINDOW:])
            outs.setdefault("swa_ks", []).append(nk)
            outs.setdefault("swa_vs", []).append(nv)
        else:
            o_p = moba_prompt(qkv, batch, seq)
            flat = (-1, PAGE_ROWS, HEAD_DIM)
            o_s = moba_sample(page_table, block_diag_queries(q_rows),
                              own_page(k_new_s).reshape(flat), own_page(v_new_s).reshape(flat),
                              cache_moba_k[a_idx].reshape(flat), cache_moba_v[a_idx].reshape(flat))
            outs.setdefault("moba_kp", []).append(k_new_p)
            outs.setdefault("moba_vp", []).append(v_new_p)
            outs.setdefault("moba_ks", []).append(k_new_s.reshape(dec_batch, dec_seq, N_KV_HEADS, HEAD_DIM))
            outs.setdefault("moba_vs", []).append(v_new_s.reshape(dec_batch, dec_seq, N_KV_HEADS, HEAD_DIM))
        o = jnp.concatenate([o_p, heads_from_rows(o_s, dec_seq).astype(BF16)], axis=0)
        x = matmul_residual(o, w_o_bf16[None], 0, x, OUT_TN, OUT_TK)
        x = half_ffn(x, 1, i)

    ms = dec_batch * dec_seq
    y_prompt = rmsnorm(x, final_norm, F32, tile=FINAL_TILE, rows=mp).reshape(batch, seq, d)
    y_sample = rmsnorm(x, final_norm, F32, tile=ms, first_tile=mp // ms, rows=ms).reshape(dec_batch, dec_seq, d)
    return (y_prompt, y_sample,
            jnp.stack(outs["swa_kp"]), jnp.stack(outs["swa_vp"]),
            jnp.stack(outs["swa_ks"]), jnp.stack(outs["swa_vs"]),
            jnp.stack(outs["moba_kp"]), jnp.stack(outs["moba_vp"]),
            jnp.stack(outs["moba_ks"]), jnp.stack(outs["moba_vs"]))
```

```python
import functools

import jax
import jax.numpy as jnp
from jax import lax
from jax.experimental import pallas as pl
from jax.experimental.pallas import tpu as pltpu

F32 = jnp.float32
BF16 = jnp.bfloat16

D_MODEL = 4096
N_HEADS = 32
N_KV_HEADS = 8
HEAD_DIM = 128
GROUP = N_HEADS // N_KV_HEADS
KV_DIM = N_KV_HEADS * HEAD_DIM
D_FF = 11008
WINDOW = 128
MOBA_BLOCK = 256
MOBA_TOPK = 3
PAGE_SIZE = 128
PAGE_ROWS = PAGE_SIZE * N_KV_HEADS
PAST_LEN = 16384
DEC_SEQ = 4
ROPE_THETA = 10000.0
RMS_EPS = 1e-6
SCALE = HEAD_DIM ** -0.5
LOG2E = 1.4426950408889634

V7X_VMEM_BYTES = 64 * 1024 * 1024
VMEM_LIMIT = V7X_VMEM_BYTES - 8 * 1024 * 1024
LANES = 128

ROW_TILE = 1040
NORM_TILE = 416
FINAL_TILE = 512
SWA_QT = 256
PROMPT_KV_HEADS = 8
GATE_ROW_TILE = 2080
FF_TILE = 256
D_FF_PAD = 11264
DOWN_TN = 1024
DOWN_TK = 2816
QKV_TN = 1024
OUT_TN = 1024
OUT_TK = 2048
PAGES_PER_CHUNK = 4
CHUNK_SLOTS = 8
ROWS_PER_KV = GROUP * DEC_SEQ
NEG_INF = float("-inf")
NT_DIMS = (((1,), (1,)), ((), ()))
TN_DIMS = (((0,), (0,)), ((), ()))


def _params(*sem):
    return pltpu.CompilerParams(dimension_semantics=sem, vmem_limit_bytes=VMEM_LIMIT)


def _rms_kernel(x_ref, g_ref, o_ref):
    x = x_ref[...]
    ms = jnp.mean(x * x, axis=-1, keepdims=True)
    o_ref[...] = ((x * lax.rsqrt(ms + RMS_EPS)) * g_ref[...]).astype(o_ref.dtype)


def rmsnorm(x, gain, out_dtype, tile=NORM_TILE, first_tile=0, rows=None):
    d = x.shape[1]
    rows = x.shape[0] if rows is None else rows
    return pl.pallas_call(
        _rms_kernel,
        grid=(rows // tile,),
        in_specs=[pl.BlockSpec((tile, d), lambda i: (first_tile + i, 0)),
                  pl.BlockSpec((1, d), lambda i: (0, 0))],
        out_specs=pl.BlockSpec((tile, d), lambda i: (i, 0)),
        out_shape=jax.ShapeDtypeStruct((rows, d), out_dtype),
        compiler_params=_params("parallel"),
        name="rmsnorm",
    )(x, gain.reshape(1, d))


def _gateup_kernel(h_ref, wg_ref, wu_ref, wd_ref, o_ref, wdb_ref, *, real_tiles):
    i = pl.program_id(0)
    j = pl.program_id(1)

    @pl.when(j < real_tiles)
    def _():
        h = h_ref[...]
        g = jnp.dot(h, wg_ref[...].astype(BF16), preferred_element_type=F32)
        u = jnp.dot(h, wu_ref[...].astype(BF16), preferred_element_type=F32)
        o_ref[...] = (0.5 * g * jax.nn.sigmoid(g) * u).astype(o_ref.dtype)

    @pl.when(j >= real_tiles)
    def _():
        o_ref[...] = jnp.zeros_like(o_ref)

    @pl.when((i == 0) & (j < real_tiles))
    def _():
        wdb_ref[...] = wd_ref[...].astype(BF16)

    @pl.when((i == 0) & (j >= real_tiles))
    def _():
        wdb_ref[...] = jnp.zeros_like(wdb_ref)


def gate_up(h, wg, wu, wd, layer):
    m, d = h.shape
    real_tiles = wg.shape[2] // FF_TILE
    pad_tiles = D_FF_PAD // FF_TILE
    assert pad_tiles == real_tiles + 1

    def real(j):
        return jnp.minimum(j, real_tiles - 1)

    w_spec = pl.BlockSpec((None, d, FF_TILE), lambda i, j: (layer, 0, real(j)))
    wd_spec = pl.BlockSpec((None, FF_TILE, d), lambda i, j: (layer, jnp.where(i == 0, real(j), real_tiles - 1), 0))
    wdb_spec = pl.BlockSpec((FF_TILE, d), lambda i, j: (jnp.where(i == 0, j, real_tiles), 0))
    return pl.pallas_call(
        functools.partial(_gateup_kernel, real_tiles=real_tiles),
        grid=(m // GATE_ROW_TILE, pad_tiles),
        in_specs=[pl.BlockSpec((GATE_ROW_TILE, d), lambda i, j: (i, 0), pipeline_mode=pl.Buffered(1)),
                  w_spec, w_spec, wd_spec],
        out_specs=[pl.BlockSpec((GATE_ROW_TILE, FF_TILE), lambda i, j: (i, j)), wdb_spec],
        out_shape=[jax.ShapeDtypeStruct((m, D_FF_PAD), BF16), jax.ShapeDtypeStruct((D_FF_PAD, d), BF16)],
        compiler_params=_params("arbitrary", "arbitrary"),
        name="gate_up",
    )(h, wg, wu, wd)


def _matmul_residual_kernel(a_ref, w_ref, r_ref, *rest, n_side):
    side_in, o_ref, side_out = rest[:n_side], rest[n_side], rest[n_side + 1:]
    k = pl.program_id(2)

    @pl.when(k == 0)
    def _():
        o_ref[...] = r_ref[...] + jnp.dot(a_ref[...], w_ref[...].astype(BF16), preferred_element_type=F32)

    @pl.when(k > 0)
    def _():
        o_ref[...] += jnp.dot(a_ref[...], w_ref[...].astype(BF16), preferred_element_type=F32)

    for src, dst in zip(side_in, side_out):
        dst[...] = src[...].astype(BF16)


def matmul_residual(a, w, layer, res, tn, tk, side=()):
    m, kdim = a.shape
    n = w.shape[2]
    gi, gj, gk = m // ROW_TILE, n // tn, kdim // tk
    steps = gi * gj * gk

    def step(i, j, k):
        return (i * gj + j) * gk + k

    side_in, side_out, side_shapes = [], [], []
    for sw, sl in side:
        rows, cols = sw.shape[1] // steps, sw.shape[2]
        assert rows * steps == sw.shape[1]
        side_in.append(pl.BlockSpec((None, rows, cols), lambda i, j, k, sl=sl: (sl, step(i, j, k), 0)))
        side_out.append(pl.BlockSpec((rows, cols), lambda i, j, k: (step(i, j, k), 0)))
        side_shapes.append(jax.ShapeDtypeStruct(sw.shape[1:], BF16))
    out = pl.pallas_call(
        functools.partial(_matmul_residual_kernel, n_side=len(side)),
        grid=(gi, gj, gk),
        in_specs=[pl.BlockSpec((ROW_TILE, tk), lambda i, j, k: (i, k)),
                  pl.BlockSpec((None, tk, tn), lambda i, j, k: (layer, k, j)),
                  pl.BlockSpec((ROW_TILE, tn), lambda i, j, k: (i, j))] + side_in,
        out_specs=[pl.BlockSpec((ROW_TILE, tn), lambda i, j, k: (i, j))] + side_out,
        out_shape=[jax.ShapeDtypeStruct((m, n), F32)] + side_shapes,
        compiler_params=_params("arbitrary", "arbitrary", "arbitrary"),
        name="matmul_residual",
    )(a, w, res, *[sw for sw, _ in side])
    return (out[0], *out[1:]) if side else out[0]


def _qkv_kernel(h_ref, w_ref, cos_ref, sin_ref, o_ref, *, rope_tiles):
    acc = jnp.dot(h_ref[...], w_ref[...].astype(BF16), preferred_element_type=F32)
    n = pl.program_id(1)

    @pl.when(n < rope_tiles)
    def _():
        cosf = cos_ref[...]
        sinf = sin_ref[...]
        for c in range(acc.shape[1] // HEAD_DIM):
            x = acc[:, c * HEAD_DIM:(c + 1) * HEAD_DIM]
            o_ref[:, c * HEAD_DIM:(c + 1) * HEAD_DIM] = x * cosf + pltpu.roll(x, HEAD_DIM // 2, 1) * sinf

    @pl.when(n >= rope_tiles)
    def _():
        o_ref[...] = acc


def qkv_rope(h, w, layer, cosf, sinf):
    m, d = h.shape
    n = w.shape[2]
    rope_tiles = (N_HEADS + N_KV_HEADS) * HEAD_DIM // QKV_TN
    return pl.pallas_call(
        functools.partial(_qkv_kernel, rope_tiles=rope_tiles),
        grid=(m // ROW_TILE, n // QKV_TN),
        in_specs=[pl.BlockSpec((ROW_TILE, d), lambda i, j: (i, 0)),
                  pl.BlockSpec((None, d, QKV_TN), lambda i, j: (layer, 0, j)),
                  pl.BlockSpec((ROW_TILE, HEAD_DIM), lambda i, j: (i, 0)),
                  pl.BlockSpec((ROW_TILE, HEAD_DIM), lambda i, j: (i, 0))],
        out_specs=pl.BlockSpec((ROW_TILE, QKV_TN), lambda i, j: (i, j)),
        out_shape=jax.ShapeDtypeStruct((m, n), F32),
        compiler_params=_params("parallel", "arbitrary"),
        name="qkv_rope",
    )(h, w, cosf, sinf)


def rope_tables(pos):
    inv = ROPE_THETA ** (-jnp.arange(0, HEAD_DIM, 2, dtype=F32) / HEAD_DIM)
    ang = pos.astype(F32)[:, None] * inv[None, :]
    cos = jnp.cos(ang)
    sin = jnp.sin(ang)
    return jnp.concatenate([cos, cos], axis=-1), jnp.concatenate([-sin, sin], axis=-1)


def _swa_prompt_kernel(sink_ref, q_ref, kp_ref, kc_ref, vp_ref, vc_ref, o_ref):
    g = pl.program_id(1)
    n = pl.program_id(2)
    qw = GROUP * HEAD_DIM
    nq = GROUP * SWA_QT
    nk = WINDOW + SWA_QT
    key = lax.broadcasted_iota(jnp.int32, (nk, nq), 0)
    query = lax.broadcasted_iota(jnp.int32, (nk, nq), 1) % SWA_QT
    mask = (key > query) & (key <= query + WINDOW) & ((key >= WINDOW) | (n > 0))
    head = lax.broadcasted_iota(jnp.int32, (1, nq), 1) // SWA_QT
    for h in range(PROMPT_KV_HEADS):
        cols = slice(h * HEAD_DIM, (h + 1) * HEAD_DIM)
        qf = jnp.concatenate([q_ref[:, h * qw + j * HEAD_DIM:h * qw + (j + 1) * HEAD_DIM] for j in range(GROUP)], axis=0)
        q = (qf * (SCALE * LOG2E)).astype(BF16)
        k = jnp.concatenate([kp_ref[:, cols], kc_ref[:, cols]], axis=0).astype(BF16)
        v = jnp.concatenate([vp_ref[:, cols], vc_ref[:, cols]], axis=0).astype(BF16)
        s = jnp.where(mask, lax.dot_general(k, q, NT_DIMS, preferred_element_type=F32), NEG_INF)
        sink = jnp.zeros((1, nq), F32)
        for j in range(GROUP):
            sink = jnp.where(head == j, sink_ref[(g * PROMPT_KV_HEADS + h) * GROUP + j] * LOG2E, sink)
        m = jnp.maximum(jnp.max(s, axis=0, keepdims=True), sink)
        p = jnp.exp2(s - m)
        denom = jnp.sum(p, axis=0, keepdims=True) + jnp.exp2(sink - m)
        out = lax.dot_general(v, p.astype(BF16), TN_DIMS, preferred_element_type=F32) / denom
        for j in range(GROUP):
            o_ref[:, h * qw + j * HEAD_DIM:h * qw + (j + 1) * HEAD_DIM] = out[:, j * SWA_QT:(j + 1) * SWA_QT].T.astype(o_ref.dtype)


def swa_prompt(qkv, sinks, batch, seq):
    nt = seq // SWA_QT
    wpt = SWA_QT // WINDOW
    kw = PROMPT_KV_HEADS * HEAD_DIM
    kcol = N_HEADS * HEAD_DIM // kw
    vcol = (N_HEADS + N_KV_HEADS) * HEAD_DIM // kw
    qw = PROMPT_KV_HEADS * GROUP * HEAD_DIM

    def prev_window(col):
        return pl.BlockSpec((WINDOW, kw), lambda b, g, n: ((b * nt + n) * wpt - jnp.minimum(n, 1), col + g))

    def tile(col):
        return pl.BlockSpec((SWA_QT, kw), lambda b, g, n: (b * nt + n, col + g))

    return pl.pallas_call(
        _swa_prompt_kernel,
        grid=(batch, N_KV_HEADS // PROMPT_KV_HEADS, nt),
        in_specs=[
            pl.BlockSpec(memory_space=pltpu.SMEM),
            pl.BlockSpec((SWA_QT, qw), lambda b, g, n: (b * nt + n, g)),
            prev_window(kcol), tile(kcol), prev_window(vcol), tile(vcol),
        ],
        out_specs=pl.BlockSpec((SWA_QT, qw), lambda b, g, n: (b * nt + n, g)),
        out_shape=jax.ShapeDtypeStruct((batch * seq, N_HEADS * HEAD_DIM), BF16),
        compiler_params=_params("parallel", "parallel", "arbitrary"),
        name="swa_prompt",
    )(sinks, qkv, qkv, qkv, qkv, qkv)


def _moba_prompt_kernel(q_ref, k_ref, v_ref, o_ref, means_ref, sel_ref, acc_ref, *, nblk):
    c = pl.program_id(2)
    qw = GROUP * HEAD_DIM

    @pl.when(c == 0)
    def _():
        for n in range(nblk):
            kb = k_ref[n * MOBA_BLOCK:(n + 1) * MOBA_BLOCK, :]
            means_ref[n:n + 1, :] = jnp.sum(kb, axis=0, keepdims=True) / MOBA_BLOCK

    qs = []
    for h in range(PROMPT_KV_HEADS):
        qf = jnp.concatenate([q_ref[:, h * qw + j * HEAD_DIM:h * qw + (j + 1) * HEAD_DIM] for j in range(GROUP)], axis=0)
        qs.append((qf * (SCALE * LOG2E)).astype(BF16))
        gate = lax.dot_general(means_ref[:, h * HEAD_DIM:(h + 1) * HEAD_DIM], qf, NT_DIMS,
                               precision=lax.Precision.HIGHEST, preferred_element_type=F32)
        blk = lax.broadcasted_iota(jnp.int32, gate.shape, 0)
        gate = jnp.where(blk < c, gate, NEG_INF)
        for n in range(nblk):
            gate_n = gate[n:n + 1, :]
            ahead = (gate > gate_n) | ((gate == gate_n) & (blk < n))
            rank = jnp.sum(jnp.where(ahead, 1.0, 0.0), axis=0, keepdims=True)
            sel_ref[h, n:n + 1, :] = jnp.where(rank < MOBA_TOPK, 1.0, 0.0)

    def block_scores(h, off):
        kb = k_ref[pl.ds(off, MOBA_BLOCK), h * HEAD_DIM:(h + 1) * HEAD_DIM].astype(BF16)
        return lax.dot_general(kb, qs[h], NT_DIMS, preferred_element_type=F32)

    def weighted_values(h, off, p):
        vb = v_ref[pl.ds(off, MOBA_BLOCK), h * HEAD_DIM:(h + 1) * HEAD_DIM].astype(BF16)
        return lax.dot_general(vb, p.astype(BF16), TN_DIMS, preferred_element_type=F32)

    own = pl.multiple_of(c * MOBA_BLOCK, MOBA_BLOCK)
    init = []
    for h in range(PROMPT_KV_HEADS):
        s = block_scores(h, own)
        key = lax.broadcasted_iota(jnp.int32, s.shape, 0)
        query = lax.broadcasted_iota(jnp.int32, s.shape, 1) % MOBA_BLOCK
        s = jnp.where(key <= query, s, NEG_INF)
        m0 = jnp.max(s, axis=0, keepdims=True)
        p = jnp.exp2(s - m0)
        init.append((m0, jnp.sum(p, axis=0, keepdims=True)))
        acc_ref[h] = weighted_values(h, own, p)

    def past_block(n, carry):
        off = pl.multiple_of(n * MOBA_BLOCK, MOBA_BLOCK)
        out = []
        for h in range(PROMPT_KV_HEADS):
            m, l = carry[h]
            sb = jnp.where(sel_ref[h, pl.ds(n, 1), :] > 0.0, block_scores(h, off), NEG_INF)
            m_new = jnp.maximum(m, jnp.max(sb, axis=0, keepdims=True))
            alpha = jnp.exp2(m - m_new)
            pb = jnp.exp2(sb - m_new)
            acc_ref[h] = alpha * acc_ref[h] + weighted_values(h, off, pb)
            out.append((m_new, alpha * l + jnp.sum(pb, axis=0, keepdims=True)))
        return tuple(out)

    final = lax.fori_loop(0, c, past_block, tuple(init))
    for h in range(PROMPT_KV_HEADS):
        out = acc_ref[h] / final[h][1]
        for j in range(GROUP):
            o_ref[:, h * qw + j * HEAD_DIM:h * qw + (j + 1) * HEAD_DIM] = out[:, j * MOBA_BLOCK:(j + 1) * MOBA_BLOCK].T.astype(o_ref.dtype)


def moba_prompt(qkv, batch, seq):
    nblk = seq // MOBA_BLOCK
    kw = PROMPT_KV_HEADS * HEAD_DIM
    kcol = N_HEADS * HEAD_DIM // kw
    vcol = (N_HEADS + N_KV_HEADS) * HEAD_DIM // kw
    qw = PROMPT_KV_HEADS * GROUP * HEAD_DIM
    nq = GROUP * MOBA_BLOCK
    return pl.pallas_call(
        functools.partial(_moba_prompt_kernel, nblk=nblk),
        grid=(batch, N_KV_HEADS // PROMPT_KV_HEADS, nblk),
        in_specs=[
            pl.BlockSpec((MOBA_BLOCK, qw), lambda b, g, c: (b * nblk + c, g)),
            pl.BlockSpec((seq, kw), lambda b, g, c: (b, kcol + g)),
            pl.BlockSpec((seq, kw), lambda b, g, c: (b, vcol + g)),
        ],
        out_specs=pl.BlockSpec((MOBA_BLOCK, qw), lambda b, g, c: (b * nblk + c, g)),
        out_shape=jax.ShapeDtypeStruct((batch * seq, N_HEADS * HEAD_DIM), BF16),
        scratch_shapes=[pltpu.VMEM((nblk, kw), F32),
                        pltpu.VMEM((PROMPT_KV_HEADS, nblk, nq), F32),
                        pltpu.VMEM((PROMPT_KV_HEADS, HEAD_DIM, nq), F32)],
        compiler_params=_params("parallel", "parallel", "arbitrary"),
        name="moba_prompt",
    )(qkv, qkv, qkv)


def _swa_sample_kernel(sink_ref, q_ref, kn_ref, vn_ref, ck_ref, cv_ref, o_ref, nk_ref, nv_ref):
    keep = WINDOW - DEC_SEQ
    nk_ref[0, 0:keep] = ck_ref[0, DEC_SEQ:WINDOW]
    nk_ref[0, keep:WINDOW] = kn_ref[0, 0:DEC_SEQ]
    nv_ref[0, 0:keep] = cv_ref[0, DEC_SEQ:WINDOW]
    nv_ref[0, keep:WINDOW] = vn_ref[0, 0:DEC_SEQ]

    t = lax.broadcasted_iota(jnp.int32, (ROWS_PER_KV, 2 * WINDOW), 0) % DEC_SEQ
    col = lax.broadcasted_iota(jnp.int32, (ROWS_PER_KV, 2 * WINDOW), 1)
    mask = ((col < WINDOW) & (col > t)) | ((col >= WINDOW) & (col - WINDOW <= t))
    row = lax.broadcasted_iota(jnp.int32, (ROWS_PER_KV, 1), 0)
    for g in range(N_KV_HEADS):
        q = q_ref[0, g].astype(BF16)
        k = jnp.concatenate([ck_ref[0, :, g, :], kn_ref[0, :, g, :]], axis=0).astype(BF16)
        v = jnp.concatenate([cv_ref[0, :, g, :], vn_ref[0, :, g, :]], axis=0).astype(BF16)
        s = lax.dot_general(q, k, NT_DIMS, preferred_element_type=F32) * SCALE
        s = jnp.where(mask, s, NEG_INF)
        sink = jnp.zeros((ROWS_PER_KV, 1), F32)
        for j in range(GROUP):
            sink = jnp.where(row // DEC_SEQ == j, sink_ref[g * GROUP + j], sink)
        m = jnp.maximum(jnp.max(s, axis=-1, keepdims=True), sink)
        p = jnp.exp(s - m)
        denom = jnp.sum(p, axis=-1, keepdims=True) + jnp.exp(sink - m)
        o_ref[0, g] = jnp.dot(p.astype(BF16), v, preferred_element_type=F32) / denom


def swa_sample(sinks, q_rows, k_own, v_own, cache_k, cache_v):
    nb = q_rows.shape[0]
    page_spec = pl.BlockSpec((1, PAGE_SIZE, N_KV_HEADS, HEAD_DIM), lambda b: (b, 0, 0, 0))
    q_spec = pl.BlockSpec((1, N_KV_HEADS, ROWS_PER_KV, HEAD_DIM), lambda b: (b, 0, 0, 0))
    cache_shape = jax.ShapeDtypeStruct((nb, WINDOW, N_KV_HEADS, HEAD_DIM), F32)
    return pl.pallas_call(
        _swa_sample_kernel,
        grid=(nb,),
        in_specs=[pl.BlockSpec(memory_space=pltpu.SMEM), q_spec, page_spec, page_spec, page_spec, page_spec],
        out_specs=[q_spec, page_spec, page_spec],
        out_shape=[jax.ShapeDtypeStruct(q_rows.shape, F32), cache_shape, cache_shape],
        compiler_params=_params("parallel"),
        name="swa_sample",
    )(sinks, q_rows, k_own, v_own, cache_k, cache_v)


def _head_concat(flat_ref, first_token, n_tokens):
    return jnp.concatenate(
        [flat_ref[pl.ds(first_token * N_KV_HEADS + g, n_tokens, stride=N_KV_HEADS), :]
         for g in range(N_KV_HEADS)], axis=1)


def _lane_to_rows(row_vec):
    i = lax.broadcasted_iota(jnp.int32, (LANES, LANES), 0)
    j = lax.broadcasted_iota(jnp.int32, (LANES, LANES), 1)
    return jnp.sum(jnp.where(i == j, row_vec, 0.0), axis=1, keepdims=True)


def _moba_sample_kernel(pt_ref, qbd_ref, kn_ref, vn_ref, kpool, vpool, o_ref,
                        buf, sem, scores, means, sel, acc, *, n_pages):
    b = pl.program_id(0)
    chunk_tokens = PAGES_PER_CHUNK * PAGE_SIZE
    n_chunks = n_pages // PAGES_PER_CHUNK
    blocks_per_chunk = chunk_tokens // MOBA_BLOCK
    n_past = n_pages * PAGE_SIZE // MOBA_BLOCK

    assert n_chunks % CHUNK_SLOTS == 0
    lookahead = CHUNK_SLOTS - 1

    def chunk_copy(pool, sample, ci, p):
        page = pt_ref[sample * n_pages + ci * PAGES_PER_CHUNK + p]
        slot = ci % CHUNK_SLOTS
        return pltpu.make_async_copy(pool.at[page], buf.at[slot, pl.ds(p * PAGE_ROWS, PAGE_ROWS)],
                                     sem.at[slot, p])

    def start_chunk(pool, sample, ci):
        for p in range(PAGES_PER_CHUNK):
            chunk_copy(pool, sample, ci, p).start()

    def wait_chunk(pool, ci):
        for p in range(PAGES_PER_CHUNK):
            chunk_copy(pool, b, ci, p).wait()

    def stream(pool, per_block, next_pool, next_sample, has_next):
        def body(ci, carry):
            @pl.when(ci + lookahead < n_chunks)
            def _():
                start_chunk(pool, b, ci + lookahead)

            @pl.when((ci + lookahead >= n_chunks) & has_next)
            def _():
                start_chunk(next_pool, next_sample, ci + lookahead - n_chunks)

            wait_chunk(pool, ci)
            flat = buf.at[ci % CHUNK_SLOTS]
            for h in range(blocks_per_chunk):
                per_block(ci * blocks_per_chunk + h, _head_concat(flat, h * MOBA_BLOCK, MOBA_BLOCK))
            return carry

        lax.fori_loop(0, n_chunks, body, 0)

    def scores_of(rows, qbd):
        return jnp.dot(rows.astype(BF16), qbd, preferred_element_type=F32) * SCALE

    def weighted_values(p, rows):
        return lax.dot_general(p.astype(BF16), rows.astype(BF16), TN_DIMS, preferred_element_type=F32)

    qbd_f32 = qbd_ref[0]
    qbd = qbd_f32.astype(BF16)

    def k_block(n, rows):
        means[pl.ds(n, 1), :] = jnp.sum(rows, axis=0, keepdims=True) / MOBA_BLOCK
        off = pl.multiple_of(n * MOBA_BLOCK, MOBA_BLOCK)
        scores[pl.ds(off, MOBA_BLOCK), :] = scores_of(rows, qbd)

    @pl.when(b == 0)
    def _():
        for ci in range(lookahead):
            start_chunk(kpool, b, ci)

    stream(kpool, k_block, vpool, b, True)

    gate = jnp.dot(means[...], qbd_f32, precision=lax.Precision.HIGHEST, preferred_element_type=F32)
    blk = lax.broadcasted_iota(jnp.int32, gate.shape, 0).astype(F32)
    chosen = jnp.zeros(gate.shape, F32)
    for _ in range(MOBA_TOPK):
        best = jnp.max(gate, axis=0, keepdims=True)
        first = jnp.min(jnp.where(gate == best, blk, float(n_past)), axis=0, keepdims=True)
        hit = blk == first
        chosen = jnp.where(hit, 1.0, chosen)
        gate = jnp.where(hit, NEG_INF, gate)
    sel[...] = chosen

    own_k = _head_concat(kn_ref.at[0], 0, PAGE_SIZE)
    own_v = _head_concat(vn_ref.at[0], 0, PAGE_SIZE)
    s_own = scores_of(own_k, qbd)
    row = lax.broadcasted_iota(jnp.int32, s_own.shape, 0)
    t = lax.broadcasted_iota(jnp.int32, s_own.shape, 1) % DEC_SEQ
    s_own = jnp.where(row <= t, s_own, NEG_INF)

    def block_scores(n):
        off = pl.multiple_of(n * MOBA_BLOCK, MOBA_BLOCK)
        return off, jnp.where(sel[pl.ds(n, 1), :] > 0.0, scores[pl.ds(off, MOBA_BLOCK), :], NEG_INF)

    def max_body(n, m):
        return jnp.maximum(m, jnp.max(block_scores(n)[1], axis=0, keepdims=True))

    m = lax.fori_loop(0, n_past, max_body, jnp.max(s_own, axis=0, keepdims=True))
    p_own = jnp.exp(s_own - m)

    def exp_body(n, l):
        off, s = block_scores(n)
        p = jnp.exp(s - m)
        scores[pl.ds(off, MOBA_BLOCK), :] = p
        return l + jnp.sum(p, axis=0, keepdims=True)

    l = lax.fori_loop(0, n_past, exp_body, jnp.sum(p_own, axis=0, keepdims=True))

    acc[...] = weighted_values(p_own, own_v)

    def v_block(n, rows):
        off = pl.multiple_of(n * MOBA_BLOCK, MOBA_BLOCK)
        acc[...] += weighted_values(scores[pl.ds(off, MOBA_BLOCK), :], rows)

    stream(vpool, v_block, kpool, b + 1, b + 1 < pl.num_programs(0))

    inv = 1.0 / _lane_to_rows(l)
    for g in range(N_KV_HEADS):
        rows = slice(g * ROWS_PER_KV, (g + 1) * ROWS_PER_KV)
        o_ref[0, g] = acc[rows, g * HEAD_DIM:(g + 1) * HEAD_DIM] * inv[rows]


def moba_sample(page_table, qbd, k_own, v_own, pool_k, pool_v):
    nb, n_pages = page_table.shape
    n_past = n_pages * PAGE_SIZE // MOBA_BLOCK
    page_spec = pl.BlockSpec((1, PAGE_ROWS, HEAD_DIM), lambda b, pt: (b, 0, 0))
    grid_spec = pltpu.PrefetchScalarGridSpec(
        num_scalar_prefetch=1,
        grid=(nb,),
        in_specs=[pl.BlockSpec((1, KV_DIM, LANES), lambda b, pt: (b, 0, 0)), page_spec, page_spec,
                  pl.BlockSpec(memory_space=pl.ANY), pl.BlockSpec(memory_space=pl.ANY)],
        out_specs=pl.BlockSpec((1, N_KV_HEADS, ROWS_PER_KV, HEAD_DIM), lambda b, pt: (b, 0, 0, 0)),
        scratch_shapes=[
            pltpu.VMEM((CHUNK_SLOTS, PAGES_PER_CHUNK * PAGE_ROWS, HEAD_DIM), F32),
            pltpu.SemaphoreType.DMA((CHUNK_SLOTS, PAGES_PER_CHUNK)),
            pltpu.VMEM((n_past * MOBA_BLOCK, LANES), F32),
            pltpu.VMEM((n_past, KV_DIM), F32),
            pltpu.VMEM((n_past, LANES), F32),
            pltpu.VMEM((LANES, KV_DIM), F32),
        ],
    )
    return pl.pallas_call(
        functools.partial(_moba_sample_kernel, n_pages=n_pages),
        grid_spec=grid_spec,
        out_shape=jax.ShapeDtypeStruct((nb, N_KV_HEADS, ROWS_PER_KV, HEAD_DIM), F32),
        compiler_params=_params("arbitrary"),
        name="moba_sample",
    )(page_table.reshape(-1), qbd, k_own, v_own, pool_k, pool_v)


def sample_query_rows(q):
    nb, t, _ = q.shape
    q = q.reshape(nb, t, N_KV_HEADS, GROUP, HEAD_DIM).transpose(0, 2, 3, 1, 4)
    return q.reshape(nb, N_KV_HEADS, GROUP * t, HEAD_DIM)


def block_diag_queries(q_rows):
    nb = q_rows.shape[0]
    eye = jnp.eye(N_KV_HEADS, dtype=q_rows.dtype)
    return jnp.einsum("bgrd,gh->bgdhr", q_rows, eye).reshape(nb, KV_DIM, N_KV_HEADS * ROWS_PER_KV)


def heads_from_rows(o, t):
    nb = o.shape[0]
    o = o.reshape(nb, N_KV_HEADS, GROUP, t, HEAD_DIM).transpose(0, 3, 1, 2, 4)
    return o.reshape(nb * t, N_HEADS * HEAD_DIM)


def own_page(x):
    nb, t, _ = x.shape
    x = x.reshape(nb, t, N_KV_HEADS, HEAD_DIM)
    return jnp.pad(x, ((0, 0), (0, PAGE_SIZE - t), (0, 0), (0, 0)))


def kernel(x_prompt, x_sample, cache_swa_k, cache_swa_v, cache_moba_k, cache_moba_v, page_table, attn_norm, w_qkv, w_o, swa_sinks, ffn1_norm, ffn1_w_gate, ffn1_w_up, ffn1_w_down, ffn2_norm, ffn2_w_gate, ffn2_w_up, ffn2_w_down, final_norm):
    batch, seq, d = x_prompt.shape
    dec_batch, dec_seq, _ = x_sample.shape
    assert dec_seq == DEC_SEQ
    mp = batch * seq
    depth = w_qkv.shape[0]
    q_dim = N_HEADS * HEAD_DIM

    x = jnp.concatenate([x_prompt.reshape(mp, d), x_sample.reshape(dec_batch * dec_seq, d)], axis=0)
    pos = jnp.concatenate([jnp.tile(jnp.arange(seq), batch), jnp.tile(PAST_LEN + jnp.arange(dec_seq), dec_batch)])
    cosf, sinf = rope_tables(pos)

    ffn = ((ffn1_norm, ffn1_w_gate, ffn1_w_up, ffn1_w_down),
           (ffn2_norm, ffn2_w_gate, ffn2_w_up, ffn2_w_down))

    def half_ffn(x, which, layer, side=()):
        norm, wg, wu, wd = ffn[which]
        a, wd_bf16 = gate_up(rmsnorm(x, norm[layer], BF16), wg, wu, wd, layer)
        return matmul_residual(a, wd_bf16[None], 0, x, DOWN_TN, DOWN_TK, side=side)

    outs = {}
    for i in range(depth):
        a_idx = i // 2
        x, w_qkv_bf16, w_o_bf16 = half_ffn(x, 0, i, side=((w_qkv, i), (w_o, i)))
        qkv = qkv_rope(rmsnorm(x, attn_norm[i], BF16), w_qkv_bf16[None], 0, cosf, sinf)
        qkv_s = qkv[mp:].reshape(dec_batch, dec_seq, -1)
        k_new_p = qkv[:mp, q_dim:q_dim + KV_DIM].reshape(batch, seq, N_KV_HEADS, HEAD_DIM)
        v_new_p = qkv[:mp, q_dim + KV_DIM:].reshape(batch, seq, N_KV_HEADS, HEAD_DIM)
        k_new_s = qkv_s[..., q_dim:q_dim + KV_DIM]
        v_new_s = qkv_s[..., q_dim + KV_DIM:]
        q_rows = sample_query_rows(qkv_s[..., :q_dim])
        if i % 2 == 0:
            o_p = swa_prompt(qkv, swa_sinks[a_idx], batch, seq)
            o_s, nk, nv = swa_sample(swa_sinks[a_idx], q_rows, own_page(k_new_s), own_page(v_new_s),
                                     cache_swa_k[a_idx], cache_swa_v[a_idx])
            outs.setdefault("swa_kp", []).append(k_new_p[:, -WINDOW:])
            outs.setdefault("swa_vp", []).append(v_new_p[:, -WINDOW:])
            outs.setdefault("swa_ks", []).append(nk)
            outs.setdefault("swa_vs", []).append(nv)
        else:
            o_p = moba_prompt(qkv, batch, seq)
            flat = (-1, PAGE_ROWS, HEAD_DIM)
            o_s = moba_sample(page_table, block_diag_queries(q_rows),
                              own_page(k_new_s).reshape(flat), own_page(v_new_s).reshape(flat),
                              cache_moba_k[a_idx].reshape(flat), cache_moba_v[a_idx].reshape(flat))
            outs.setdefault("moba_kp", []).append(k_new_p)
            outs.setdefault("moba_vp", []).append(v_new_p)
            outs.setdefault("moba_ks", []).append(k_new_s.reshape(dec_batch, dec_seq, N_KV_HEADS, HEAD_DIM))
            outs.setdefault("moba_vs", []).append(v_new_s.reshape(dec_batch, dec_seq, N_KV_HEADS, HEAD_DIM))
        o = jnp.concatenate([o_p, heads_from_rows(o_s, dec_seq).astype(BF16)], axis=0)
        x = matmul_residual(o, w_o_bf16[None], 0, x, OUT_TN, OUT_TK)
        x = half_ffn(x, 1, i)

    ms = dec_batch * dec_seq
    y_prompt = rmsnorm(x, final_norm, F32, tile=FINAL_TILE, rows=mp).reshape(batch, seq, d)
    y_sample = rmsnorm(x, final_norm, F32, tile=ms, first_tile=mp // ms, rows=ms).reshape(dec_batch, dec_seq, d)
    return (y_prompt, y_sample,
            jnp.stack(outs["swa_kp"]), jnp.stack(outs["swa_vp"]),
            jnp.stack(outs["swa_ks"]), jnp.stack(outs["swa_vs"]),
            jnp.stack(outs["moba_kp"]), jnp.stack(outs["moba_vp"]),
            jnp.stack(outs["moba_ks"]), jnp.stack(outs["moba_vs"]))
```
